```python
import math
import jax
import jax.numpy as jnp
from jax import lax
import numpy as np

D_MODEL = 2048
BATCH = 4
SEQ = 4096
DEPTH = 4

N_BRANCHES = 4
BRANCH_WIDTH = D_MODEL // 4
GLA_HEADS = 4
GLA_DV = BRANCH_WIDTH // GLA_HEADS
GLA_DK = GLA_DV // 2
GLA_RANK = 16
GLA_GATE_NORM = 16.0
HGRN_HEADS = 4
HGRN_EXPAND = 128
HGRN_DV = BRANCH_WIDTH // HGRN_HEADS
RET_HEADS = 4
RET_DV = BRANCH_WIDTH // RET_HEADS
RET_DK = RET_DV // 2
ROPE_BASE = 10000.0
LRU_WIDTH = BRANCH_WIDTH
LRU_BLOCKS = 4
CONV_WIDTH = 4
LRU_C = 8.0
GATED_CHUNK = 16
RET_CHUNK = 64
N_EXPERTS = 32
N_GROUPS = 4
EXPERTS_PER_GROUP = N_EXPERTS // N_GROUPS
TOP_K = 2
D_EXPERT = 704
EXPERT_BLOCK = 128
DEEPNORM_ALPHA = (2 * DEPTH) ** 0.25
DEEPNORM_BETA = (8 * DEPTH) ** -0.25
NORM_EPS = 1e-5

SPLIT_SIZES = (
    GLA_HEADS * GLA_DK, GLA_HEADS * GLA_DK, GLA_HEADS * GLA_DV, GLA_HEADS * GLA_DV, GLA_RANK,
    HGRN_HEADS * HGRN_EXPAND, HGRN_HEADS * HGRN_EXPAND, HGRN_HEADS * HGRN_DV, HGRN_HEADS * HGRN_DV,
    RET_HEADS * RET_DK, RET_HEADS * RET_DK, RET_HEADS * RET_DV, RET_HEADS * RET_DV,
    LRU_WIDTH, LRU_WIDTH,
)
IN_COLS = sum(SPLIT_SIZES)

kernel_name = "hybrid_gla_hgrn2_retnet_rglru_moe"

F32 = jnp.float32


def layer_norm(x, g, b):
    xf = x.astype(F32)
    mu = jnp.mean(xf, axis=-1, keepdims=True)
    xc = xf - mu
    var = jnp.mean(xc * xc, axis=-1, keepdims=True)
    return (xc * lax.rsqrt(var + NORM_EPS) * g + b).astype(x.dtype)


def rms_norm(x, w=None):
    xf = x.astype(F32)
    y = xf * lax.rsqrt(jnp.mean(xf * xf, axis=-1, keepdims=True) + NORM_EPS)
    if w is not None:
        y = y * w
    return y.astype(x.dtype)


def to_heads(t, n_heads):
    b, s, _ = t.shape
    return t.reshape(b, s, n_heads, -1).transpose(0, 2, 1, 3)


def from_heads(t):
    b, h, s, d = t.shape
    return t.transpose(0, 2, 1, 3).reshape(b, s, h * d)


def apply_rotary(t, positions):
    half = t.shape[-1] // 2
    inv_freq = ROPE_BASE ** (-jnp.arange(half, dtype=F32) / half)
    ang = positions.astype(F32)[:, None, :, None] * inv_freq
    cos, sin = jnp.cos(ang), jnp.sin(ang)
    tf = t.astype(F32)
    t1, t2 = tf[..., :half], tf[..., half:]
    return jnp.concatenate([t1 * cos - t2 * sin, t1 * sin + t2 * cos], axis=-1).astype(t.dtype)


def inter_chunk_scan(q_dec, k_dec, v, chunk_decay):
    b, h, _, _, dk = q_dec.shape
    dv = v.shape[-1]

    def step(state, inp):
        q_c, k_c, v_c, d_c = inp
        o_c = jnp.einsum('bhid,bhdv->bhiv', q_c, state)
        state = d_c[..., None] * state + jnp.einsum('bhjd,bhjv->bhdv', k_c, v_c)
        return state, o_c

    xs = tuple(jnp.moveaxis(t, 2, 0) for t in (q_dec, k_dec, v, chunk_decay))
    _, o = lax.scan(step, jnp.zeros((b, h, dk, dv), F32), xs)
    return jnp.moveaxis(o, 0, 2)


def chunk_gated_linear_attention(q, k, v, log_a, chunk):
    b, h, s, _ = q.shape
    n = s // chunk
    q, k, v, log_a = [t.astype(F32).reshape(b, h, n, chunk, -1) for t in (q, k, v, log_a)]
    cum = jnp.cumsum(log_a, axis=3)
    cum_last = cum[:, :, :, -1:, :]
    q_dec = q * jnp.exp(cum)
    k_intra = k * jnp.exp(-cum)
    k_state = k * jnp.exp(cum_last - cum)
    causal = jnp.tril(jnp.ones((chunk, chunk), dtype=bool))
    scores = jnp.where(causal, jnp.einsum('bhnid,bhnjd->bhnij', q_dec, k_intra), 0.0)
    o_intra = jnp.einsum('bhnij,bhnjv->bhniv', scores, v)
    o_inter = inter_chunk_scan(q_dec, k_state, v, jnp.exp(cum_last[:, :, :, 0, :]))
    return (o_intra + o_inter).reshape(b, h, s, -1)


def chunk_retention(q, k, v, log_gamma, chunk):
    b, h, s, dk = q.shape
    n = s // chunk
    q, k, v = [t.astype(F32).reshape(b, h, n, chunk, -1) for t in (q, k, v)]
    pos = jnp.arange(chunk, dtype=F32)
    diff = pos[:, None] - pos[None, :]
    causal = diff >= 0
    decay = jnp.where(causal, jnp.exp(jnp.where(causal, diff, 0.0)[None] * log_gamma[:, None, None]), 0.0)
    scores = jnp.einsum('bhnid,bhnjd->bhnij', q, k) * decay[None, :, None]
    o_intra = jnp.einsum('bhnij,bhnjv->bhniv', scores, v)
    q_dec = q * jnp.exp((pos + 1.0)[None, :] * log_gamma[:, None])[None, :, None, :, None]
    k_dec = k * jnp.exp((chunk - 1.0 - pos)[None, :] * log_gamma[:, None])[None, :, None, :, None]
    chunk_decay = jnp.broadcast_to(jnp.exp(chunk * log_gamma)[None, :, None, None], (b, h, n, dk))
    o_inter = inter_chunk_scan(q_dec, k_dec, v, chunk_decay)
    return (o_intra + o_inter).reshape(b, h, s, -1)


def gla_branch(q, k, v, g, lr, w_gate2, b_gate2, norm_w):
    q = to_heads(q, GLA_HEADS) * GLA_DK ** -0.5
    k = to_heads(k, GLA_HEADS)
    v = to_heads(v, GLA_HEADS)
    gate_logits = (lr @ w_gate2 + b_gate2).astype(F32)
    log_a = to_heads(jax.nn.log_sigmoid(gate_logits) / GLA_GATE_NORM, GLA_HEADS)
    o = chunk_gated_linear_attention(q, k, v, log_a, GATED_CHUNK).astype(v.dtype)
    o = rms_norm(o, norm_w) * jax.nn.silu(to_heads(g, GLA_HEADS))
    return from_heads(o)


def hgrn2_branch(q, f, i, g, lb, norm_w):
    lb = lb.reshape(HGRN_HEADS, 1, HGRN_EXPAND)
    z = to_heads(f, HGRN_HEADS).astype(F32)
    log_f = jnp.logaddexp(jnp.log(lb), jnp.log1p(-lb) + jax.nn.log_sigmoid(z))
    k = (1.0 - lb) * jax.nn.sigmoid(-z)
    q = jax.nn.silu(to_heads(q, HGRN_HEADS))
    v = to_heads(i, HGRN_HEADS)
    o = chunk_gated_linear_attention(q, k, v, log_f, GATED_CHUNK).astype(v.dtype)
    o = rms_norm(o, norm_w) * jax.nn.silu(to_heads(g, HGRN_HEADS))
    return from_heads(o)


def retention_branch(q, k, v, g, positions):
    q = apply_rotary(to_heads(q, RET_HEADS), positions)
    k = apply_rotary(to_heads(k, RET_HEADS), positions) * RET_DK ** -0.5
    v = to_heads(v, RET_HEADS)
    log_gamma = jnp.log(1.0 - 2.0 ** (-5.0 - jnp.arange(RET_HEADS, dtype=F32)))
    o = chunk_retention(q, k, v, log_gamma, RET_CHUNK).astype(v.dtype)
    o = rms_norm(o) * jax.nn.silu(to_heads(g, RET_HEADS))
    return from_heads(o)


def rglru_branch(xb, yb, conv_w, conv_b, w_a, b_a, w_x, b_x, lam):
    xc = lax.conv_general_dilated(xb, conv_w[:, None, :], (1,), [(CONV_WIDTH - 1, 0)],
                                  dimension_numbers=('NWC', 'WIO', 'NWC'),
                                  feature_group_count=LRU_WIDTH) + conv_b
    b, s, w = xc.shape
    xr = xc.reshape(b, s, LRU_BLOCKS, -1)
    r = jax.nn.sigmoid(jnp.einsum('btgi,gij->btgj', xr, w_a).reshape(b, s, w) + b_a)
    ig = jax.nn.sigmoid(jnp.einsum('btgi,gij->btgj', xr, w_x).reshape(b, s, w) + b_x)
    log_a = -LRU_C * r.astype(F32) * jax.nn.softplus(-lam.astype(F32))
    a = jnp.exp(log_a)
    u = jnp.sqrt(-jnp.expm1(2.0 * log_a)) * (ig * xc).astype(F32)

    def combine(lhs, rhs):
        a1, b1 = lhs
        a2, b2 = rhs
        return a1 * a2, a2 * b1 + b2

    _, hs = lax.associative_scan(combine, (a, u), axis=1)
    return hs.astype(xb.dtype) * jax.nn.gelu(yb)


def route(x2d, w_router, router_bias):
    n = x2d.shape[0]
    scores = jax.nn.sigmoid((x2d @ w_router).astype(F32))
    sel = scores + router_bias.astype(F32)
    grouped = sel.reshape(n, N_GROUPS, EXPERTS_PER_GROUP)
    group_score = jnp.sum(lax.top_k(grouped, 2)[0], axis=-1)
    grp = jnp.argmax(group_score, axis=-1)
    in_group = jnp.take_along_axis(grouped, grp[:, None, None], axis=1)[:, 0]
    _, local = lax.top_k(in_group, TOP_K)
    idx = grp[:, None] * EXPERTS_PER_GROUP + local
    wts = jnp.take_along_axis(scores, idx, axis=1)
    wts = wts / jnp.sum(wts, axis=-1, keepdims=True)
    return idx, wts


def moe_ffn(x2d, idx, wts, w_gate, w_up, w_down):
    n, d = x2d.shape
    e_count = w_gate.shape[0]
    k = idx.shape[1]
    m = n * k
    e_flat = idx.reshape(-1)
    tok_flat = jnp.repeat(jnp.arange(n, dtype=jnp.int32), k)
    w_flat = wts.reshape(-1).astype(x2d.dtype)
    order = jnp.argsort(e_flat)
    e_sorted = e_flat[order]
    counts = jnp.bincount(e_flat, length=e_count)
    padded = (counts + EXPERT_BLOCK - 1) // EXPERT_BLOCK * EXPERT_BLOCK
    pad_end = jnp.cumsum(padded)
    pad_start = pad_end - padded
    raw_start = jnp.cumsum(counts) - counts
    rank = jnp.arange(m) - raw_start[e_sorted]
    dest = pad_start[e_sorted] + rank
    n_rows = (m + e_count * (EXPERT_BLOCK - 1) + EXPERT_BLOCK - 1) // EXPERT_BLOCK * EXPERT_BLOCK
    n_blocks = n_rows // EXPERT_BLOCK
    row_tok = jnp.full((n_rows,), n, dtype=jnp.int32).at[dest].set(tok_flat[order])
    row_w = jnp.zeros((n_rows,), x2d.dtype).at[dest].set(w_flat[order])
    block_e = jnp.minimum(jnp.searchsorted(pad_end, jnp.arange(n_blocks) * EXPERT_BLOCK, side='right'),
                          e_count - 1)
    x_pad = jnp.concatenate([x2d, jnp.zeros((1, d), x2d.dtype)], axis=0)
    xb = x_pad[row_tok].reshape(n_blocks, EXPERT_BLOCK, d)

    def expert_block(args):
        xblk, e = args
        hid = jax.nn.silu(xblk @ w_gate[e]) * (xblk @ w_up[e])
        return hid @ w_down[e]

    yb = lax.map(expert_block, (xb, block_e)).reshape(n_rows, d)
    y = jnp.zeros((n + 1, d), x2d.dtype).at[row_tok].add(yb * row_w[:, None])
    return y[:n]


def setup_inputs(seed: int = 0) -> dict:
    key = jax.random.key(seed)
    ks = jax.random.split(key, 32)

    def normal(k, shape, scale):
        return jax.random.normal(k, shape, F32) * scale

    blk = LRU_WIDTH // LRU_BLOCKS
    x = normal(ks[0], (BATCH, SEQ, D_MODEL), 1.0)
    offset = jax.random.randint(ks[1], (BATCH, 1), 0, 1024, dtype=jnp.int32)
    positions = offset + jnp.arange(SEQ, dtype=jnp.int32)[None, :]
    w_in = normal(ks[2], (DEPTH, D_MODEL, IN_COLS), D_MODEL ** -0.5)
    w_gla_gate2 = normal(ks[3], (DEPTH, GLA_RANK, GLA_HEADS * GLA_DK), GLA_RANK ** -0.5)
    b_gla_gate2 = normal(ks[4], (DEPTH, GLA_HEADS * GLA_DK), 0.1)
    gla_norm = 1.0 + normal(ks[5], (DEPTH, GLA_DV), 0.02)
    hgrn_lb = normal(ks[6], (DEPTH, HGRN_HEADS * HGRN_EXPAND), 0.1)
    hgrn_norm = 1.0 + normal(ks[7], (DEPTH, HGRN_DV), 0.02)
    conv_w = normal(ks[8], (DEPTH, CONV_WIDTH, LRU_WIDTH), CONV_WIDTH ** -0.5)
    conv_b = normal(ks[9], (DEPTH, LRU_WIDTH), 0.02)
    w_lru_a = normal(ks[10], (DEPTH, LRU_BLOCKS, blk, blk), blk ** -0.5)
    b_lru_a = normal(ks[11], (DEPTH, LRU_WIDTH), 0.02)
    w_lru_x = normal(ks[12], (DEPTH, LRU_BLOCKS, blk, blk), blk ** -0.5)
    b_lru_x = normal(ks[13], (DEPTH, LRU_WIDTH), 0.02)
    u = jax.random.uniform(ks[14], (DEPTH, LRU_WIDTH), F32, 0.9, 0.999)
    a = u ** (1.0 / LRU_C)
    lru_lambda = jnp.log(a) - jnp.log1p(-a)
    w_branch_up = normal(ks[15], (DEPTH, N_BRANCHES, BRANCH_WIDTH, D_MODEL), BRANCH_WIDTH ** -0.5)
    w_merge_gate = normal(ks[16], (DEPTH, N_BRANCHES, D_MODEL, D_MODEL), D_MODEL ** -0.5)
    b_merge_gate = normal(ks[17], (DEPTH, N_BRANCHES, D_MODEL), 0.02)
    w_out = normal(ks[18], (DEPTH, D_MODEL, D_MODEL), DEEPNORM_BETA * D_MODEL ** -0.5)
    ln1_g = 1.0 + normal(ks[19], (DEPTH, D_MODEL), 0.02)
    ln1_b = normal(ks[20], (DEPTH, D_MODEL), 0.02)
    w_router = normal(ks[21], (D_MODEL, N_EXPERTS), D_MODEL ** -0.5)
    router_bias = normal(ks[22], (N_EXPERTS,), 0.01)
    w_exp_gate = normal(ks[23], (DEPTH, N_EXPERTS, D_MODEL, D_EXPERT), D_MODEL ** -0.5)
    w_exp_up = normal(ks[24], (DEPTH, N_EXPERTS, D_MODEL, D_EXPERT), D_MODEL ** -0.5)
    w_exp_down = normal(ks[25], (DEPTH, N_EXPERTS, D_EXPERT, D_MODEL), DEEPNORM_BETA * D_EXPERT ** -0.5)
    ln2_g = 1.0 + normal(ks[26], (DEPTH, D_MODEL), 0.02)
    ln2_b = normal(ks[27], (DEPTH, D_MODEL), 0.02)
    return {
        "x": x, "positions": positions, "w_in": w_in,
        "w_gla_gate2": w_gla_gate2, "b_gla_gate2": b_gla_gate2, "gla_norm": gla_norm,
        "hgrn_lb": hgrn_lb, "hgrn_norm": hgrn_norm,
        "conv_w": conv_w, "conv_b": conv_b, "w_lru_a": w_lru_a, "b_lru_a": b_lru_a,
        "w_lru_x": w_lru_x, "b_lru_x": b_lru_x, "lru_lambda": lru_lambda,
        "w_branch_up": w_branch_up, "w_merge_gate": w_merge_gate, "b_merge_gate": b_merge_gate,
        "w_out": w_out, "ln1_g": ln1_g, "ln1_b": ln1_b,
        "w_router": w_router, "router_bias": router_bias,
        "w_exp_gate": w_exp_gate, "w_exp_up": w_exp_up, "w_exp_down": w_exp_down,
        "ln2_g": ln2_g, "ln2_b": ln2_b,
    }


def reference(x, positions, w_in, w_gla_gate2, b_gla_gate2, gla_norm, hgrn_lb, hgrn_norm,
              conv_w, conv_b, w_lru_a, b_lru_a, w_lru_x, b_lru_x, lru_lambda,
              w_branch_up, w_merge_gate, b_merge_gate, w_out, ln1_g, ln1_b,
              w_router, router_bias, w_exp_gate, w_exp_up, w_exp_down, ln2_g, ln2_b):
    b, s, d = x.shape
    lb_cum = jnp.cumsum(jax.nn.softmax(hgrn_lb.astype(F32), axis=0), axis=0)
    lower_bounds = lb_cum - lb_cum[:1]
    split_points = tuple(int(p) for p in np.cumsum(SPLIT_SIZES)[:-1])
    for l in range(DEPTH):
        hcat = x @ w_in[l]
        (gq, gk, gv, gg, glr, hq, hf, hi, hg, rq, rk, rv, rg, lx, ly) = jnp.split(hcat, split_points, axis=-1)
        o_gla = gla_branch(gq, gk, gv, gg, glr, w_gla_gate2[l], b_gla_gate2[l], gla_norm[l])
        o_hgrn = hgrn2_branch(hq, hf, hi, hg, lower_bounds[l], hgrn_norm[l])
        o_ret = retention_branch(rq, rk, rv, rg, positions)
        o_lru = rglru_branch(lx, ly, conv_w[l], conv_b[l], w_lru_a[l], b_lru_a[l],
                             w_lru_x[l], b_lru_x[l], lru_lambda[l])
        merged = jnp.zeros_like(x)
        for n_br, br in enumerate((o_gla, o_hgrn, o_ret, o_lru)):
            gate = jax.nn.sigmoid(x @ w_merge_gate[l, n_br] + b_merge_gate[l, n_br])
            merged = merged + gate * (br @ w_branch_up[l, n_br])
        x = layer_norm(DEEPNORM_ALPHA * x + merged @ w_out[l], ln1_g[l], ln1_b[l])
        x2d = x.reshape(b * s, d)
        idx, wts = route(x2d, w_router, router_bias)
        y = moe_ffn(x2d, idx, wts, w_exp_gate[l], w_exp_up[l], w_exp_down[l]).reshape(b, s, d)
        x = layer_norm(DEEPNORM_ALPHA * x + y, ln2_g[l], ln2_b[l])
    return x
```

```python
import functools
import math

import numpy as np
import jax
import jax.numpy as jnp
from jax import lax
from jax.experimental import pallas as pl
from jax.experimental.pallas import tpu as pltpu

F32 = jnp.float32
BF16 = jnp.bfloat16

D_MODEL = 2048
N_HEADS = 4
DV = 128
BRANCH_WIDTH = 512
GLA_DK = 64
GLA_RANK = 16
GLA_GATE_NORM = 16.0
HGRN_DK = 128
RET_DK = 64
ROPE_BASE = 10000.0
CONV_WIDTH = 4
LRU_C = 8.0
N_EXPERTS = 32
N_GROUPS = 4
EXPERTS_PER_GROUP = 8
D_EXPERT = 704
NORM_EPS = 1e-5

LANES = 128
SUBLANES = 8
VMEM_LIMIT = 56 * 1024 * 1024

_ORIG_SIZES = (("gq", 256), ("gk", 256), ("gv", 512), ("gg", 512), ("glr", 16),
               ("hq", 512), ("hf", 512), ("hi", 512), ("hg", 512),
               ("rq", 256), ("rk", 256), ("rv", 512), ("rg", 512), ("lx", 512), ("ly", 512))
_ORIG = {}
_o = 0
for _n, _w in _ORIG_SIZES:
    _ORIG[_n] = (_o, _o + _w)
    _o += _w

_SEG512 = ("gv", "gg", "hq", "hf", "hi", "hg", "rv", "rg", "lx", "ly")
_SEG256 = ("gq", "gk", "rq", "rk", "rqp", "rkp")
_OFF = {}
_o = 0
for _n in _SEG512:
    _OFF[_n] = _o
    _o += 512
for _n in _SEG256:
    _OFF[_n] = _o
    _o += 256
_OFF["glr"] = _o
_o += LANES
H_COLS = 6912
_H_PAD = H_COLS - _o

GATED_BLOCK = 128
RET_BLOCK = 256
LRU_BLOCK = 256
MOE_BLOCK = 256
GATHER_ROWS = 512


def _tile(n, pref):
    return pref if n % pref == 0 else n


def _cparams(sem):
    return pltpu.CompilerParams(dimension_semantics=sem, vmem_limit_bytes=VMEM_LIMIT)


def _dot(a, b):
    return jnp.dot(a, b, preferred_element_type=F32)


def _dot_nt(a, b):
    return lax.dot_general(a, b, (((1,), (1,)), ((), ())), preferred_element_type=F32)


def _dot_tn(a, b):
    return lax.dot_general(a, b, (((0,), (0,)), ((), ())), preferred_element_type=F32)


def _sigmoid(x):
    return 1.0 / (1.0 + jnp.exp(-x))


def _log_sigmoid(x):
    return jnp.minimum(x, 0.0) - jnp.log1p(jnp.exp(-jnp.abs(x)))


def _softplus(x):
    return jnp.maximum(x, 0.0) + jnp.log1p(jnp.exp(-jnp.abs(x)))


def _layer_norm(y, g, b):
    mu = jnp.mean(y, axis=-1, keepdims=True)
    yc = y - mu
    var = jnp.mean(yc * yc, axis=-1, keepdims=True)
    return yc * lax.rsqrt(var + NORM_EPS) * g + b


def _mm_kernel(x_ref, w_ref, o_ref):
    o_ref[...] = _dot(x_ref[...], w_ref[...]).astype(o_ref.dtype)


def _in_proj(xb, w_all, layer):
    n, k = xb.shape
    cols = w_all.shape[-1]
    tm = _tile(n, 512)
    tn = 2304
    return pl.pallas_call(
        _mm_kernel,
        grid=(cols // tn, n // tm),
        in_specs=[pl.BlockSpec((tm, k), lambda j, i: (i, 0)),
                  pl.BlockSpec((None, k, tn), lambda j, i: (layer, 0, j))],
        out_specs=pl.BlockSpec((tm, tn), lambda j, i: (i, j)),
        out_shape=jax.ShapeDtypeStruct((n, cols), F32),
        compiler_params=_cparams(("arbitrary", "arbitrary")),
    )(xb, w_all)


def _gated_consts(c):
    n16 = c // 16
    i = np.arange(c)
    g = i // 16
    mats = []
    hs = []
    h = 1
    while h <= n16:
        hs.append(h)
        h *= 2
    for h in hs:
        grp = g // h
        same = grp[:, None] == grp[None, :]
        mats.append(same & (i[None, :] <= i[:, None]))
        mats.append(same & (i[None, :] > i[:, None]))
    cum = np.concatenate(mats, 0).astype(np.float32)
    lv = -np.ones((c, c), np.int32)
    lv[(g[:, None] == g[None, :]) & (i[None, :] <= i[:, None])] = 0
    gi = g[:, None]
    gj = g[None, :]
    for li, h in enumerate(hs[:-1]):
        m = (gi // (2 * h) == gj // (2 * h)) & ((gi // h) % 2 == 1) & ((gj // h) % 2 == 0)
        lv[m] = li + 1
    return jnp.asarray(cum, BF16), jnp.asarray(lv)


def _exact_select_dot(sel, x):
    hi = x.astype(BF16)
    r1 = x - hi.astype(F32)
    mid = r1.astype(BF16)
    lo = (r1 - mid.astype(F32)).astype(BF16)
    return _dot(sel, hi) + _dot(sel, mid) + _dot(sel, lo)


def _gated_core(q, k, la, v_ref, g_ref, nw_ref, cum_ref, lv_ref, o_ref, st_ref, *, c, dk):
    ex = _exact_select_dot(cum_ref[...], la)
    nl = ex.shape[0] // (2 * c)
    c_t = [ex[(2 * l) * c:(2 * l + 1) * c] for l in range(nl)]
    c_e = [ex[(2 * l + 1) * c:(2 * l + 2) * c] for l in range(nl)]
    qd = [(q * jnp.exp(c_t[l])).astype(BF16) for l in range(nl)]
    k_diag = (k * jnp.exp(-c_t[0])).astype(BF16)
    kd = [(k * jnp.exp(c_e[l])).astype(BF16) for l in range(nl)]
    dec = jnp.exp(c_t[nl - 1][c - 1:c, :])
    lv = lv_ref[...]
    for h in range(N_HEADS):
        sk = slice(h * dk, (h + 1) * dk)
        sv = slice(h * DV, (h + 1) * DV)
        p = jnp.where(lv == 0, _dot_nt(qd[0][:, sk], k_diag[:, sk]), 0.0)
        for l in range(nl - 1):
            p = p + jnp.where(lv == l + 1, _dot_nt(qd[l][:, sk], kd[l][:, sk]), 0.0)
        vh = v_ref[:, sv].astype(BF16)
        st = st_ref[h]
        o = _dot(p.astype(BF16), vh) + _dot_nt(qd[nl - 1][:, sk], st.astype(BF16))
        st_ref[h] = st * dec[:, sk] + _dot_tn(vh, kd[nl - 1][:, sk])
        ms = jnp.mean(o * o, axis=-1, keepdims=True)
        gh = g_ref[:, sv]
        o_ref[:, sv] = (o * lax.rsqrt(ms + NORM_EPS) * nw_ref[:, sv] * (gh * _sigmoid(gh))).astype(o_ref.dtype)


def _gla_kernel(q_ref, k_ref, v_ref, g_ref, lr_ref, w2_ref, b2_ref, nw_ref, cum_ref, lv_ref,
                o_ref, st_ref, *, c):
    @pl.when(pl.program_id(1) == 0)
    def _():
        st_ref[...] = jnp.zeros_like(st_ref)

    q = q_ref[...] * (GLA_DK ** -0.5)
    logits = _dot(lr_ref[...].astype(BF16), w2_ref[...]) + b2_ref[...]
    la = _log_sigmoid(logits) * (1.0 / GLA_GATE_NORM)
    _gated_core(q, k_ref[...], la, v_ref, g_ref, nw_ref, cum_ref, lv_ref, o_ref, st_ref, c=c, dk=GLA_DK)


def _hgrn_kernel(q_ref, f_ref, v_ref, g_ref, llb_ref, l1m_ref, oml_ref, nw_ref, cum_ref, lv_ref,
                 o_ref, st_ref, *, c):
    @pl.when(pl.program_id(1) == 0)
    def _():
        st_ref[...] = jnp.zeros_like(st_ref)

    z = f_ref[...]
    a = llb_ref[...]
    b = l1m_ref[...] + _log_sigmoid(z)
    la = jnp.maximum(a, b) + jnp.log1p(jnp.exp(-jnp.abs(a - b)))
    k = oml_ref[...] * _sigmoid(-z)
    qq = q_ref[...]
    q = qq * _sigmoid(qq)
    _gated_core(q, k, la, v_ref, g_ref, nw_ref, cum_ref, lv_ref, o_ref, st_ref, c=c, dk=HGRN_DK)


def _seq_grid(n, seq, c):
    nb = seq // c
    return (n // seq, nb), nb


def _col_spec(c, width, name, nb):
    cb = _OFF[name] // width
    return pl.BlockSpec((c, width), lambda b, j: (b * nb + j, cb))


def _row_spec(width):
    return pl.BlockSpec((1, width), lambda b, j: (0, 0))


def _const_spec(shape):
    nd = len(shape)
    return pl.BlockSpec(shape, lambda b, j: (0,) * nd)


def _gla_branch(hcat, seq, w2, b2, nw, cum, lv):
    n = hcat.shape[0]
    c = _tile(seq, GATED_BLOCK)
    grid, nb = _seq_grid(n, seq, c)
    return pl.pallas_call(
        functools.partial(_gla_kernel, c=c),
        grid=grid,
        in_specs=[_col_spec(c, 256, "gq", nb), _col_spec(c, 256, "gk", nb),
                  _col_spec(c, 512, "gv", nb), _col_spec(c, 512, "gg", nb),
                  _col_spec(c, LANES, "glr", nb),
                  _const_spec(w2.shape), _row_spec(256), _row_spec(512),
                  _const_spec(cum.shape), _const_spec(lv.shape)],
        out_specs=pl.BlockSpec((c, BRANCH_WIDTH), lambda b, j: (b * nb + j, 0)),
        out_shape=jax.ShapeDtypeStruct((n, BRANCH_WIDTH), BF16),
        scratch_shapes=[pltpu.VMEM((N_HEADS, DV, GLA_DK), F32)],
        compiler_params=_cparams(("arbitrary", "arbitrary")),
    )(hcat, hcat, hcat, hcat, hcat, w2, b2, nw, cum, lv)


def _hgrn_branch(hcat, seq, llb, l1m, oml, nw, cum, lv):
    n = hcat.shape[0]
    c = _tile(seq, GATED_BLOCK)
    grid, nb = _seq_grid(n, seq, c)
    return pl.pallas_call(
        functools.partial(_hgrn_kernel, c=c),
        grid=grid,
        in_specs=[_col_spec(c, 512, "hq", nb), _col_spec(c, 512, "hf", nb),
                  _col_spec(c, 512, "hi", nb), _col_spec(c, 512, "hg", nb),
                  _row_spec(512), _row_spec(512), _row_spec(512), _row_spec(512),
                  _const_spec(cum.shape), _const_spec(lv.shape)],
        out_specs=pl.BlockSpec((c, BRANCH_WIDTH), lambda b, j: (b * nb + j, 0)),
        out_shape=jax.ShapeDtypeStruct((n, BRANCH_WIDTH), BF16),
        scratch_shapes=[pltpu.VMEM((N_HEADS, DV, HGRN_DK), F32)],
        compiler_params=_cparams(("arbitrary", "arbitrary")),
    )(hcat, hcat, hcat, hcat, llb, l1m, oml, nw, cum, lv)


def _rope_kernel(pos_ref, invf_ref, cos_ref, sin_ref):
    ang = pos_ref[...].astype(F32) * invf_ref[...]
    cos_ref[...] = jnp.cos(ang)
    sin_ref[...] = jnp.sin(ang)


def _rope_tables(pos_col, invf):
    n = pos_col.shape[0]
    tm = _tile(n, 2048)
    w = invf.shape[1]
    return pl.pallas_call(
        _rope_kernel,
        grid=(n // tm,),
        in_specs=[pl.BlockSpec((tm, 1), lambda i: (i, 0)), pl.BlockSpec((1, w), lambda i: (0, 0))],
        out_specs=[pl.BlockSpec((tm, w), lambda i: (i, 0))] * 2,
        out_shape=[jax.ShapeDtypeStruct((n, w), F32)] * 2,
        compiler_params=_cparams(("arbitrary",)),
    )(pos_col, invf)


def _ret_consts(c):
    hh = np.arange(N_HEADS, dtype=np.float64)
    log_gamma = np.log(1.0 - 2.0 ** (-5.0 - hh))
    pos = np.arange(c, dtype=np.float64)
    diff = pos[:, None] - pos[None, :]
    dm = np.where(diff >= 0, np.exp(np.where(diff >= 0, diff, 0.0)[None] * log_gamma[:, None, None]), 0.0)
    col_head = np.arange(N_HEADS * RET_DK) // RET_DK
    qs = np.exp((pos + 1.0)[:, None] * log_gamma[col_head][None, :])
    ks = np.exp((c - 1.0 - pos)[:, None] * log_gamma[col_head][None, :])
    gc = np.exp(c * log_gamma[col_head])[None, :]
    return (jnp.asarray(dm, F32), jnp.asarray(qs, F32), jnp.asarray(ks, F32), jnp.asarray(gc, F32))


def _ret_kernel(q_ref, k_ref, qp_ref, kp_ref, v_ref, g_ref, cos_ref, sin_ref, dm_ref, qs_ref, ks_ref,
                gc_ref, o_ref, st_ref):
    @pl.when(pl.program_id(1) == 0)
    def _():
        st_ref[...] = jnp.zeros_like(st_ref)

    cs = cos_ref[...]
    sn = sin_ref[...]
    q = q_ref[...] * cs + qp_ref[...] * sn
    k = (k_ref[...] * cs + kp_ref[...] * sn) * (RET_DK ** -0.5)
    qb = q.astype(BF16)
    kb = k.astype(BF16)
    q_dec = (q * qs_ref[...]).astype(BF16)
    k_dec = (k * ks_ref[...]).astype(BF16)
    gc = gc_ref[...]
    for h in range(N_HEADS):
        sk = slice(h * RET_DK, (h + 1) * RET_DK)
        sv = slice(h * DV, (h + 1) * DV)
        s = _dot_nt(qb[:, sk], kb[:, sk]) * dm_ref[h]
        vh = v_ref[:, sv].astype(BF16)
        st = st_ref[h]
        o = _dot(s.astype(BF16), vh) + _dot_nt(q_dec[:, sk], st.astype(BF16))
        st_ref[h] = st * gc[:, sk] + _dot_tn(vh, k_dec[:, sk])
        ms = jnp.mean(o * o, axis=-1, keepdims=True)
        gh = g_ref[:, sv]
        o_ref[:, sv] = (o * lax.rsqrt(ms + NORM_EPS) * (gh * _sigmoid(gh))).astype(o_ref.dtype)


def _ret_branch(hcat, seq, cos_t, sin_t, consts, c):
    n = hcat.shape[0]
    grid, nb = _seq_grid(n, seq, c)
    dm, qs, ks, gc = consts
    tab = pl.BlockSpec((c, 256), lambda b, j: (b * nb + j, 0))
    return pl.pallas_call(
        _ret_kernel,
        grid=grid,
        in_specs=[_col_spec(c, 256, "rq", nb), _col_spec(c, 256, "rk", nb),
                  _col_spec(c, 256, "rqp", nb), _col_spec(c, 256, "rkp", nb),
                  _col_spec(c, 512, "rv", nb), _col_spec(c, 512, "rg", nb), tab, tab,
                  _const_spec(dm.shape), _const_spec(qs.shape), _const_spec(ks.shape), _row_spec(256)],
        out_specs=pl.BlockSpec((c, BRANCH_WIDTH), lambda b, j: (b * nb + j, 0)),
        out_shape=jax.ShapeDtypeStruct((n, BRANCH_WIDTH), BF16),
        scratch_shapes=[pltpu.VMEM((N_HEADS, DV, RET_DK), F32)],
        compiler_params=_cparams(("arbitrary", "arbitrary")),
    )(hcat, hcat, hcat, hcat, hcat, hcat, cos_t, sin_t, dm, qs, ks, gc)


def _lru_kernel(x_ref, y_ref, cw_ref, cb_ref, wa_ref, ba_ref, wx_ref, bx_ref, lam_ref, o_ref,
                halo_ref, h_ref, *, c):
    @pl.when(pl.program_id(1) == 0)
    def _():
        halo_ref[...] = jnp.zeros_like(halo_ref)
        h_ref[...] = jnp.zeros_like(h_ref)

    x = x_ref[...]
    x_ext = jnp.concatenate([halo_ref[...], x], axis=0)
    cw = cw_ref[...]
    xc = cw[CONV_WIDTH - 1:CONV_WIDTH, :] * x
    for w in range(CONV_WIDTH - 1):
        shift = CONV_WIDTH - 1 - w
        xc = xc + cw[w:w + 1, :] * pltpu.roll(x_ext, shift, 0)[SUBLANES:, :]
    xc = xc + cb_ref[...]
    halo_ref[...] = x[c - SUBLANES:, :]

    blk = BRANCH_WIDTH // 4
    ra = []
    rx = []
    for g in range(4):
        xg = xc[:, g * blk:(g + 1) * blk].astype(BF16)
        ra.append(_dot(xg, wa_ref[g]))
        rx.append(_dot(xg, wx_ref[g]))
    r = _sigmoid(jnp.concatenate(ra, axis=1) + ba_ref[...])
    ig = _sigmoid(jnp.concatenate(rx, axis=1) + bx_ref[...])
    log_a = -LRU_C * r * _softplus(-lam_ref[...])
    a = jnp.exp(log_a)
    u = jnp.sqrt(-jnp.tanh(log_a) * (a * a + 1.0)) * (ig * xc)

    rows = lax.broadcasted_iota(jnp.int32, (c, 1), 0)
    s = 1
    while s < c:
        keep = rows >= s
        a_sh = jnp.where(keep, pltpu.roll(a, s, 0), 1.0)
        u_sh = jnp.where(keep, pltpu.roll(u, s, 0), 0.0)
        u = a * u_sh + u
        a = a * a_sh
        s *= 2
    hs = u + a * h_ref[0:1, :]
    h_ref[...] = jnp.broadcast_to(hs[c - 1:c, :], h_ref.shape)
    y = y_ref[...]
    gelu = 0.5 * y * (1.0 + jnp.tanh(math.sqrt(2.0 / math.pi) * (y + 0.044715 * (y * y * y))))
    o_ref[...] = (hs * gelu).astype(o_ref.dtype)


def _lru_branch(hcat, seq, cw, cb, wa, ba, wx, bx, lam):
    n = hcat.shape[0]
    c = _tile(seq, LRU_BLOCK)
    grid, nb = _seq_grid(n, seq, c)
    return pl.pallas_call(
        functools.partial(_lru_kernel, c=c),
        grid=grid,
        in_specs=[_col_spec(c, 512, "lx", nb), _col_spec(c, 512, "ly", nb),
                  _const_spec(cw.shape), _row_spec(512), _const_spec(wa.shape), _row_spec(512),
                  _const_spec(wx.shape), _row_spec(512), _row_spec(512)],
        out_specs=pl.BlockSpec((c, BRANCH_WIDTH), lambda b, j: (b * nb + j, 0)),
        out_shape=jax.ShapeDtypeStruct((n, BRANCH_WIDTH), BF16),
        scratch_shapes=[pltpu.VMEM((SUBLANES, BRANCH_WIDTH), F32), pltpu.VMEM((SUBLANES, BRANCH_WIDTH), F32)],
        compiler_params=_cparams(("arbitrary", "arbitrary")),
    )(hcat, hcat, cw, cb, wa, ba, wx, bx, lam)


def _merge_kernel(x_ref, o0_ref, o1_ref, o2_ref, o3_ref, wg_ref, wu_ref, b_ref, out_ref):
    x = x_ref[...]
    acc = None
    for br, o_ref in enumerate((o0_ref, o1_ref, o2_ref, o3_ref)):
        gate = _sigmoid(_dot(x, wg_ref[br]) + b_ref[br])
        term = gate * _dot(o_ref[...], wu_ref[br])
        acc = term if acc is None else acc + term
    out_ref[...] = acc.astype(out_ref.dtype)


def _merge(xb, outs, wg_all, wu_all, b_all, layer):
    n = xb.shape[0]
    tm = _tile(n, 1024)
    tn = 512
    ospec = pl.BlockSpec((tm, BRANCH_WIDTH), lambda i, j: (i, 0))
    return pl.pallas_call(
        _merge_kernel,
        grid=(n // tm, D_MODEL // tn),
        in_specs=[pl.BlockSpec((tm, D_MODEL), lambda i, j: (i, 0)), ospec, ospec, ospec, ospec,
                  pl.BlockSpec((None, 4, D_MODEL, tn), lambda i, j: (layer, 0, 0, j)),
                  pl.BlockSpec((None, 4, BRANCH_WIDTH, tn), lambda i, j: (layer, 0, 0, j)),
                  pl.BlockSpec((None, 4, 1, tn), lambda i, j: (layer, 0, 0, j))],
        out_specs=pl.BlockSpec((tm, tn), lambda i, j: (i, j)),
        out_shape=jax.ShapeDtypeStruct((n, D_MODEL), BF16),
        compiler_params=_cparams(("arbitrary", "arbitrary")),
    )(xb, *outs, wg_all, wu_all, b_all)


def _outln_kernel(m_ref, w_ref, x_ref, g_ref, b_ref, wr_ref, x1_ref, lg_ref, *, alpha):
    y = alpha * x_ref[...] + _dot(m_ref[...], w_ref[...])
    x1 = _layer_norm(y, g_ref[...], b_ref[...])
    x1_ref[...] = x1
    lg_ref[...] = _dot_nt(wr_ref[...], x1.astype(BF16))


def _out_proj_ln(merged, w_all, x, g_all, b_all, wr_t, layer, alpha):
    n = x.shape[0]
    tm = _tile(n, 512)
    row = pl.BlockSpec((tm, D_MODEL), lambda i: (i, 0))
    vec = pl.BlockSpec((None, 1, D_MODEL), lambda i: (layer, 0, 0))
    return pl.pallas_call(
        functools.partial(_outln_kernel, alpha=alpha),
        grid=(n // tm,),
        in_specs=[row, pl.BlockSpec((None, D_MODEL, D_MODEL), lambda i: (layer, 0, 0)), row, vec, vec,
                  pl.BlockSpec((N_EXPERTS, D_MODEL), lambda i: (0, 0))],
        out_specs=[row, pl.BlockSpec((N_EXPERTS, tm), lambda i: (0, i))],
        out_shape=[jax.ShapeDtypeStruct((n, D_MODEL), F32), jax.ShapeDtypeStruct((N_EXPERTS, n), F32)],
        compiler_params=_cparams(("arbitrary",)),
    )(merged, w_all, x, g_all, b_all, wr_t)


def _route_kernel(lg_ref, bias_ref, idx_ref, wt_ref):
    s = _sigmoid(lg_ref[...])
    sel = s + bias_ref[...]
    tn = s.shape[1]
    rows = lax.broadcasted_iota(jnp.int32, (EXPERTS_PER_GROUP, tn), 0)
    neg = jnp.float32(-jnp.inf)

    def first_max(blk):
        m = jnp.max(blk, axis=0, keepdims=True)
        i = jnp.min(jnp.where(blk == m, rows, EXPERTS_PER_GROUP), axis=0, keepdims=True)
        return m, i

    best = None
    for g in range(N_GROUPS):
        sl = slice(g * EXPERTS_PER_GROUP, (g + 1) * EXPERTS_PER_GROUP)
        blk = sel[sl]
        sg = s[sl]
        m1, i1 = first_max(blk)
        m2, i2 = first_max(jnp.where(rows == i1, neg, blk))
        w1 = jnp.sum(jnp.where(rows == i1, sg, 0.0), axis=0, keepdims=True)
        w2 = jnp.sum(jnp.where(rows == i2, sg, 0.0), axis=0, keepdims=True)
        cand = (m1 + m2, i1 + g * EXPERTS_PER_GROUP, i2 + g * EXPERTS_PER_GROUP, w1, w2)
        if best is None:
            best = cand
        else:
            take = cand[0] > best[0]
            best = tuple(jnp.where(take, cn, bs) for cn, bs in zip(cand, best))
    _, e1, e2, w1, w2 = best
    tot = w1 + w2
    idx_ref[0:1, :] = e1
    idx_ref[1:2, :] = e2
    wt_ref[0:1, :] = w1 / tot
    wt_ref[1:2, :] = w2 / tot


def _route(logits_t, bias_col):
    n = logits_t.shape[1]
    tn = _tile(n, 2048)
    return pl.pallas_call(
        _route_kernel,
        grid=(n // tn,),
        in_specs=[pl.BlockSpec((N_EXPERTS, tn), lambda i: (0, i)),
                  pl.BlockSpec((N_EXPERTS, 1), lambda i: (0, 0))],
        out_specs=[pl.BlockSpec((2, tn), lambda i: (0, i))] * 2,
        out_shape=[jax.ShapeDtypeStruct((2, n), jnp.int32), jax.ShapeDtypeStruct((2, n), F32)],
        compiler_params=_cparams(("arbitrary",)),
    )(logits_t, bias_col)


def _gather_kernel(idx_ref, src_ref, dst_ref, sem, *, rows):
    base = pl.program_id(0) * rows

    def row_copy(r):
        return pltpu.make_async_copy(src_ref.at[pl.ds(idx_ref[base + r], 1)],
                                     dst_ref.at[pl.ds(base + r, 1)], sem)

    def start(r, carry):
        row_copy(r).start()
        return carry

    def wait(r, carry):
        row_copy(r).wait()
        return carry

    lax.fori_loop(0, rows, start, 0)
    lax.fori_loop(0, rows, wait, 0)


def _gather_rows(src, idx):
    m = idx.shape[0]
    rows = _tile(m, GATHER_ROWS)
    return pl.pallas_call(
        functools.partial(_gather_kernel, rows=rows),
        grid_spec=pltpu.PrefetchScalarGridSpec(
            num_scalar_prefetch=1,
            grid=(m // rows,),
            in_specs=[pl.BlockSpec(memory_space=pl.ANY)],
            out_specs=pl.BlockSpec(memory_space=pl.ANY),
            scratch_shapes=[pltpu.SemaphoreType.DMA(())],
        ),
        out_shape=jax.ShapeDtypeStruct((m, src.shape[1]), src.dtype),
        compiler_params=_cparams(("arbitrary",)),
    )(idx, src)


def _moe_kernel(be_ref, nb_ref, x_ref, rw_ref, wg_ref, wu_ref, wd_ref, y_ref):
    i = pl.program_id(0)

    @pl.when(i < nb_ref[0])
    def _():
        x = x_ref[...].astype(BF16)
        kc = D_MODEL // 4
        g = None
        u = None
        for j in range(4):
            xs = x[:, j * kc:(j + 1) * kc]
            gj = _dot(xs, wg_ref[j * kc:(j + 1) * kc, :].astype(BF16))
            uj = _dot(xs, wu_ref[j * kc:(j + 1) * kc, :].astype(BF16))
            g = gj if g is None else g + gj
            u = uj if u is None else u + uj
        hid = ((g * _sigmoid(g)) * u).astype(BF16)
        y = _dot(hid, wd_ref[...].astype(BF16))
        y_ref[...] = y * rw_ref[...]

    @pl.when(i >= nb_ref[0])
    def _():
        y_ref[...] = jnp.zeros_like(y_ref)


def _moe_ffn(xs, row_w, block_e, n_used, wg_all, wu_all, wd_all, layer):
    n_rows = xs.shape[0]
    bm = MOE_BLOCK
    wspec_in = pl.BlockSpec((None, None, D_MODEL, D_EXPERT), lambda i, be, nb: (layer, be[i], 0, 0))
    wspec_out = pl.BlockSpec((None, None, D_EXPERT, D_MODEL), lambda i, be, nb: (layer, be[i], 0, 0))
    return pl.pallas_call(
        _moe_kernel,
        grid_spec=pltpu.PrefetchScalarGridSpec(
            num_scalar_prefetch=2,
            grid=(n_rows // bm,),
            in_specs=[pl.BlockSpec((bm, D_MODEL), lambda i, be, nb: (i, 0)),
                      pl.BlockSpec((bm, 1), lambda i, be, nb: (i, 0)),
                      wspec_in, wspec_in, wspec_out],
            out_specs=pl.BlockSpec((bm, D_MODEL), lambda i, be, nb: (i, 0)),
        ),
        out_shape=jax.ShapeDtypeStruct((n_rows, D_MODEL), F32),
        compiler_params=_cparams(("arbitrary",)),
    )(block_e, n_used, xs, row_w, wg_all, wu_all, wd_all)


def _combine_kernel(x_ref, y0_ref, y1_ref, g_ref, b_ref, o_ref, ob_ref, *, alpha):
    y = alpha * x_ref[...] + (y0_ref[...] + y1_ref[...])
    x2 = _layer_norm(y, g_ref[...], b_ref[...])
    o_ref[...] = x2
    ob_ref[...] = x2.astype(BF16)


def _combine_ln(x1, y2, g_all, b_all, layer, alpha):
    n = x1.shape[0]
    tm = _tile(n, 512)
    nt = n // tm
    row = pl.BlockSpec((tm, D_MODEL), lambda i: (i, 0))
    vec = pl.BlockSpec((None, 1, D_MODEL), lambda i: (layer, 0, 0))
    return pl.pallas_call(
        functools.partial(_combine_kernel, alpha=alpha),
        grid=(nt,),
        in_specs=[row, row, pl.BlockSpec((tm, D_MODEL), lambda i: (i + nt, 0)), vec, vec],
        out_specs=[row, row],
        out_shape=[jax.ShapeDtypeStruct((n, D_MODEL), F32), jax.ShapeDtypeStruct((n, D_MODEL), BF16)],
        compiler_params=_cparams(("arbitrary",)),
    )(x1, y2, y2, g_all, b_all)


def _dispatch_plan(idx_t, wt_t, n):
    m = 2 * n
    bm = MOE_BLOCK
    e_flat = idx_t.reshape(m)
    tok = jnp.tile(jnp.arange(n, dtype=jnp.int32), 2)
    onehot = (e_flat[:, None] == jnp.arange(N_EXPERTS, dtype=jnp.int32)[None, :]).astype(jnp.int32)
    csum = jnp.cumsum(onehot, axis=0)
    rank = jnp.sum(csum * onehot, axis=1) - 1
    counts = csum[-1]
    padded = (counts + bm - 1) // bm * bm
    pad_end = jnp.cumsum(padded)
    pad_start = pad_end - padded
    dest = (pad_start[e_flat] + rank).astype(jnp.int32)
    n_rows = (m + N_EXPERTS * (bm - 1) + bm - 1) // bm * bm
    n_blocks = n_rows // bm
    row_tok = jnp.zeros((n_rows,), jnp.int32).at[dest].set(tok)
    row_w = jnp.zeros((n_rows,), F32).at[dest].set(wt_t.reshape(m))
    block_e = jnp.minimum(jnp.searchsorted(pad_end, jnp.arange(n_blocks, dtype=jnp.int32) * bm, side="right"),
                          N_EXPERTS - 1).astype(jnp.int32)
    n_used = (pad_end[-1] // bm).astype(jnp.int32).reshape(1)
    return row_tok, row_w.reshape(n_rows, 1), block_e, n_used, dest


def _relayout_w_in(w_in):
    def seg(name):
        a, b = _ORIG[name]
        return w_in[..., a:b]

    def partner(name):
        a0 = _ORIG[name][0]
        half = RET_DK // 2
        parts = []
        for h in range(N_HEADS):
            b = a0 + RET_DK * h
            parts += [-w_in[..., b + half:b + RET_DK], w_in[..., b:b + half]]
        return parts

    lead = w_in.shape[:-1]
    cols = ([seg(nm) for nm in _SEG512] + [seg("gq"), seg("gk"), seg("rq"), seg("rk")]
            + partner("rq") + partner("rk")
            + [seg("glr"), jnp.zeros(lead + (LANES - GLA_RANK + _H_PAD,), w_in.dtype)])
    return jnp.concatenate(cols, axis=-1).astype(BF16)


def kernel(x, positions, w_in, w_gla_gate2, b_gla_gate2, gla_norm, hgrn_lb, hgrn_norm, conv_w, conv_b,
           w_lru_a, b_lru_a, w_lru_x, b_lru_x, lru_lambda, w_branch_up, w_merge_gate, b_merge_gate, w_out,
           ln1_g, ln1_b, w_router, router_bias, w_exp_gate, w_exp_up, w_exp_down, ln2_g, ln2_b):
    bsz, seq, d = x.shape
    depth = w_in.shape[0]
    n = bsz * seq
    alpha = (2 * depth) ** 0.25

    w_in_b = _relayout_w_in(w_in)
    w2_b = jnp.pad(w_gla_gate2, ((0, 0), (0, LANES - GLA_RANK), (0, 0))).astype(BF16)
    b2 = b_gla_gate2.reshape(depth, 1, -1)
    gla_nw = jnp.tile(gla_norm, (1, N_HEADS)).reshape(depth, 1, -1)
    hgrn_nw = jnp.tile(hgrn_norm, (1, N_HEADS)).reshape(depth, 1, -1)
    lb_cum = jnp.cumsum(jax.nn.softmax(hgrn_lb.astype(F32), axis=0), axis=0)
    lower = lb_cum - lb_cum[:1]
    log_lb = jnp.log(lower).reshape(depth, 1, -1)
    log_1m_lb = jnp.log1p(-lower).reshape(depth, 1, -1)
    one_m_lb = (1.0 - lower).reshape(depth, 1, -1)
    wa_b = w_lru_a.astype(BF16)
    wx_b = w_lru_x.astype(BF16)
    wg_b = w_merge_gate.astype(BF16)
    wu_b = w_branch_up.astype(BF16)
    bmg = b_merge_gate.reshape(depth, 4, 1, d)
    w_out_b = w_out.astype(BF16)
    wr_t = w_router.T.astype(BF16)
    bias_col = router_bias.reshape(N_EXPERTS, 1).astype(F32)
    ln1g = ln1_g.reshape(depth, 1, d)
    ln1b = ln1_b.reshape(depth, 1, d)
    ln2g = ln2_g.reshape(depth, 1, d)
    ln2b = ln2_b.reshape(depth, 1, d)

    half = RET_DK // 2
    inv_freq = ROPE_BASE ** (-jnp.arange(half, dtype=F32) / half)
    invf = jnp.tile(inv_freq, 2 * N_HEADS).reshape(1, N_HEADS * RET_DK)
    cos_t, sin_t = _rope_tables(positions.reshape(n, 1), invf)

    cum, lv = _gated_consts(_tile(seq, GATED_BLOCK))
    ret_c = _tile(seq, RET_BLOCK)
    ret_consts = _ret_consts(ret_c)

    x2d = x.reshape(n, d)
    xb = x2d.astype(BF16)
    for l in range(depth):
        hcat = _in_proj(xb, w_in_b, l)
        o_gla = _gla_branch(hcat, seq, w2_b[l], b2[l], gla_nw[l], cum, lv)
        o_hgrn = _hgrn_branch(hcat, seq, log_lb[l], log_1m_lb[l], one_m_lb[l], hgrn_nw[l], cum, lv)
        o_ret = _ret_branch(hcat, seq, cos_t, sin_t, ret_consts, ret_c)
        o_lru = _lru_branch(hcat, seq, conv_w[l], conv_b[l].reshape(1, -1), wa_b[l], b_lru_a[l].reshape(1, -1),
                            wx_b[l], b_lru_x[l].reshape(1, -1), lru_lambda[l].reshape(1, -1))
        merged = _merge(xb, (o_gla, o_hgrn, o_ret, o_lru), wg_b, wu_b, bmg, l)
        x1, logits_t = _out_proj_ln(merged, w_out_b, x2d, ln1g, ln1b, wr_t, l, alpha)
        idx_t, wt_t = _route(logits_t, bias_col)
        row_tok, row_w, block_e, n_used, dest = _dispatch_plan(idx_t, wt_t, n)
        xs = _gather_rows(x1, row_tok)
        yb = _moe_ffn(xs, row_w, block_e, n_used, w_exp_gate, w_exp_up, w_exp_down, l)
        y2 = _gather_rows(yb, dest)
        x2d, xb = _combine_ln(x1, y2, ln2g, ln2b, l, alpha)
    return x2d.reshape(bsz, seq, d)
```

```python
import functools
import math

import numpy as np
import jax
import jax.numpy as jnp
from jax import lax
from jax.experimental import pallas as pl
from jax.experimental.pallas import tpu as pltpu

F32 = jnp.float32
BF16 = jnp.bfloat16

D_MODEL = 2048
N_HEADS = 4
DV = 128
BRANCH_WIDTH = 512
GLA_DK = 64
GLA_RANK = 16
GLA_GATE_NORM = 16.0
HGRN_DK = 128
RET_DK = 64
ROPE_BASE = 10000.0
CONV_WIDTH = 4
LRU_C = 8.0
N_EXPERTS = 32
N_GROUPS = 4
EXPERTS_PER_GROUP = 8
D_EXPERT = 704
NORM_EPS = 1e-5

LANES = 128
SUBLANES = 8
ROW_CHUNKS = D_MODEL // LANES
VMEM_LIMIT = 56 * 1024 * 1024

_ORIG_SIZES = (("gq", 256), ("gk", 256), ("gv", 512), ("gg", 512), ("glr", 16),
               ("hq", 512), ("hf", 512), ("hi", 512), ("hg", 512),
               ("rq", 256), ("rk", 256), ("rv", 512), ("rg", 512), ("lx", 512), ("ly", 512))
_ORIG = {}
_o = 0
for _n, _w in _ORIG_SIZES:
    _ORIG[_n] = (_o, _o + _w)
    _o += _w

_SEG512 = ("gv", "gg", "hq", "hf", "hi", "hg", "rv", "rg", "lx", "ly")
_SEG256 = ("gq", "gk", "rq", "rk", "rqp", "rkp")
_OFF = {}
_o = 0
for _n in _SEG512:
    _OFF[_n] = _o
    _o += 512
for _n in _SEG256:
    _OFF[_n] = _o
    _o += 256
_OFF["glr"] = _o
_o += LANES
H_COLS = 6912
_H_PAD = H_COLS - _o

GATED_BLOCK = 128
RET_BLOCK = 256
LRU_BLOCK = 256
MOE_BLOCK = 256
SCATTER_TILE = 512
COMBINE_TILE = 256
RANK_TILE = 1024


def _tile(n, pref):
    return pref if n % pref == 0 else n


def _cparams(sem):
    return pltpu.CompilerParams(dimension_semantics=sem, vmem_limit_bytes=VMEM_LIMIT)


def _dot(a, b):
    return jnp.dot(a, b, preferred_element_type=F32)


def _dot_nt(a, b):
    return lax.dot_general(a, b, (((1,), (1,)), ((), ())), preferred_element_type=F32)


def _dot_tn(a, b):
    return lax.dot_general(a, b, (((0,), (0,)), ((), ())), preferred_element_type=F32)


def _sigmoid(x):
    return 1.0 / (1.0 + jnp.exp(-x))


def _log_sigmoid(x):
    return jnp.minimum(x, 0.0) - jnp.log1p(jnp.exp(-jnp.abs(x)))


def _softplus(x):
    return jnp.maximum(x, 0.0) + jnp.log1p(jnp.exp(-jnp.abs(x)))


def _layer_norm(y, g, b):
    mu = jnp.mean(y, axis=-1, keepdims=True)
    yc = y - mu
    var = jnp.mean(yc * yc, axis=-1, keepdims=True)
    return yc * lax.rsqrt(var + NORM_EPS) * g + b


def _mm_kernel(x_ref, w_ref, o_ref):
    o_ref[...] = _dot(x_ref[...], w_ref[...]).astype(o_ref.dtype)


def _in_proj(xb, w_all, layer):
    n, k = xb.shape
    cols = w_all.shape[-1]
    tm = _tile(n, 512)
    tn = 2304
    return pl.pallas_call(
        _mm_kernel,
        grid=(cols // tn, n // tm),
        in_specs=[pl.BlockSpec((tm, k), lambda j, i: (i, 0)),
                  pl.BlockSpec((None, k, tn), lambda j, i: (layer, 0, j))],
        out_specs=pl.BlockSpec((tm, tn), lambda j, i: (i, j)),
        out_shape=jax.ShapeDtypeStruct((n, cols), F32),
        compiler_params=_cparams(("arbitrary", "arbitrary")),
    )(xb, w_all)


def _gated_consts(c):
    n16 = c // 16
    i = np.arange(c)
    g = i // 16
    mats = []
    hs = []
    h = 1
    while h <= n16:
        hs.append(h)
        h *= 2
    for h in hs:
        grp = g // h
        same = grp[:, None] == grp[None, :]
        mats.append(same & (i[None, :] <= i[:, None]))
        mats.append(same & (i[None, :] > i[:, None]))
    cum = np.concatenate(mats, 0).astype(np.float32)
    lv = -np.ones((c, c), np.int32)
    lv[(g[:, None] == g[None, :]) & (i[None, :] <= i[:, None])] = 0
    gi = g[:, None]
    gj = g[None, :]
    for li, h in enumerate(hs[:-1]):
        m = (gi // (2 * h) == gj // (2 * h)) & ((gi // h) % 2 == 1) & ((gj // h) % 2 == 0)
        lv[m] = li + 1
    return jnp.asarray(cum, BF16), jnp.asarray(lv)


def _exact_select_dot(sel, x):
    hi = x.astype(BF16)
    r1 = x - hi.astype(F32)
    mid = r1.astype(BF16)
    lo = (r1 - mid.astype(F32)).astype(BF16)
    return _dot(sel, hi) + _dot(sel, mid) + _dot(sel, lo)


def _gated_core(q, k, la, v_ref, g_ref, nw_ref, cum_ref, lv_ref, o_ref, st_ref, *, c, dk):
    ex = _exact_select_dot(cum_ref[...], la)
    nl = ex.shape[0] // (2 * c)
    c_t = [ex[(2 * l) * c:(2 * l + 1) * c] for l in range(nl)]
    c_e = [ex[(2 * l + 1) * c:(2 * l + 2) * c] for l in range(nl)]
    qd = [(q * jnp.exp(c_t[l])).astype(BF16) for l in range(nl)]
    k_diag = (k * jnp.exp(-c_t[0])).astype(BF16)
    kd = [(k * jnp.exp(c_e[l])).astype(BF16) for l in range(nl)]
    dec = jnp.exp(c_t[nl - 1][c - 1:c, :])
    lv = lv_ref[...]
    for h in range(N_HEADS):
        sk = slice(h * dk, (h + 1) * dk)
        sv = slice(h * DV, (h + 1) * DV)
        p = jnp.where(lv == 0, _dot_nt(qd[0][:, sk], k_diag[:, sk]), 0.0)
        for l in range(nl - 1):
            p = p + jnp.where(lv == l + 1, _dot_nt(qd[l][:, sk], kd[l][:, sk]), 0.0)
        vh = v_ref[:, sv].astype(BF16)
        st = st_ref[h]
        o = _dot(p.astype(BF16), vh) + _dot_nt(qd[nl - 1][:, sk], st.astype(BF16))
        st_ref[h] = st * dec[:, sk] + _dot_tn(vh, kd[nl - 1][:, sk])
        ms = jnp.mean(o * o, axis=-1, keepdims=True)
        gh = g_ref[:, sv]
        o_ref[:, sv] = (o * lax.rsqrt(ms + NORM_EPS) * nw_ref[:, sv] * (gh * _sigmoid(gh))).astype(o_ref.dtype)


def _gla_kernel(q_ref, k_ref, v_ref, g_ref, lr_ref, w2_ref, b2_ref, nw_ref, cum_ref, lv_ref,
                o_ref, st_ref, *, c):
    @pl.when(pl.program_id(1) == 0)
    def _():
        st_ref[...] = jnp.zeros_like(st_ref)

    q = q_ref[...] * (GLA_DK ** -0.5)
    logits = _dot(lr_ref[...].astype(BF16), w2_ref[...]) + b2_ref[...]
    la = _log_sigmoid(logits) * (1.0 / GLA_GATE_NORM)
    _gated_core(q, k_ref[...], la, v_ref, g_ref, nw_ref, cum_ref, lv_ref, o_ref, st_ref, c=c, dk=GLA_DK)


def _hgrn_kernel(q_ref, f_ref, v_ref, g_ref, llb_ref, l1m_ref, oml_ref, nw_ref, cum_ref, lv_ref,
                 o_ref, st_ref, *, c):
    @pl.when(pl.program_id(1) == 0)
    def _():
        st_ref[...] = jnp.zeros_like(st_ref)

    z = f_ref[...]
    a = llb_ref[...]
    b = l1m_ref[...] + _log_sigmoid(z)
    la = jnp.maximum(a, b) + jnp.log1p(jnp.exp(-jnp.abs(a - b)))
    k = oml_ref[...] * _sigmoid(-z)
    qq = q_ref[...]
    q = qq * _sigmoid(qq)
    _gated_core(q, k, la, v_ref, g_ref, nw_ref, cum_ref, lv_ref, o_ref, st_ref, c=c, dk=HGRN_DK)


def _seq_grid(n, seq, c):
    nb = seq // c
    return (n // seq, nb), nb


def _col_spec(c, width, name, nb):
    cb = _OFF[name] // width
    return pl.BlockSpec((c, width), lambda b, j: (b * nb + j, cb))


def _row_spec(width):
    return pl.BlockSpec((1, width), lambda b, j: (0, 0))


def _const_spec(shape):
    nd = len(shape)
    return pl.BlockSpec(shape, lambda b, j: (0,) * nd)


def _gla_branch(hcat, seq, w2, b2, nw, cum, lv):
    n = hcat.shape[0]
    c = _tile(seq, GATED_BLOCK)
    grid, nb = _seq_grid(n, seq, c)
    return pl.pallas_call(
        functools.partial(_gla_kernel, c=c),
        grid=grid,
        in_specs=[_col_spec(c, 256, "gq", nb), _col_spec(c, 256, "gk", nb),
                  _col_spec(c, 512, "gv", nb), _col_spec(c, 512, "gg", nb),
                  _col_spec(c, LANES, "glr", nb),
                  _const_spec(w2.shape), _row_spec(256), _row_spec(512),
                  _const_spec(cum.shape), _const_spec(lv.shape)],
        out_specs=pl.BlockSpec((c, BRANCH_WIDTH), lambda b, j: (b * nb + j, 0)),
        out_shape=jax.ShapeDtypeStruct((n, BRANCH_WIDTH), BF16),
        scratch_shapes=[pltpu.VMEM((N_HEADS, DV, GLA_DK), F32)],
        compiler_params=_cparams(("arbitrary", "arbitrary")),
    )(hcat, hcat, hcat, hcat, hcat, w2, b2, nw, cum, lv)


def _hgrn_branch(hcat, seq, llb, l1m, oml, nw, cum, lv):
    n = hcat.shape[0]
    c = _tile(seq, GATED_BLOCK)
    grid, nb = _seq_grid(n, seq, c)
    return pl.pallas_call(
        functools.partial(_hgrn_kernel, c=c),
        grid=grid,
        in_specs=[_col_spec(c, 512, "hq", nb), _col_spec(c, 512, "hf", nb),
                  _col_spec(c, 512, "hi", nb), _col_spec(c, 512, "hg", nb),
                  _row_spec(512), _row_spec(512), _row_spec(512), _row_spec(512),
                  _const_spec(cum.shape), _const_spec(lv.shape)],
        out_specs=pl.BlockSpec((c, BRANCH_WIDTH), lambda b, j: (b * nb + j, 0)),
        out_shape=jax.ShapeDtypeStruct((n, BRANCH_WIDTH), BF16),
        scratch_shapes=[pltpu.VMEM((N_HEADS, DV, HGRN_DK), F32)],
        compiler_params=_cparams(("arbitrary", "arbitrary")),
    )(hcat, hcat, hcat, hcat, llb, l1m, oml, nw, cum, lv)


def _rope_kernel(pos_ref, invf_ref, cos_ref, sin_ref):
    ang = pos_ref[...].astype(F32) * invf_ref[...]
    cos_ref[...] = jnp.cos(ang)
    sin_ref[...] = jnp.sin(ang)


def _rope_tables(pos_col, invf):
    n = pos_col.shape[0]
    tm = _tile(n, 2048)
    w = invf.shape[1]
    return pl.pallas_call(
        _rope_kernel,
        grid=(n // tm,),
        in_specs=[pl.BlockSpec((tm, 1), lambda i: (i, 0)), pl.BlockSpec((1, w), lambda i: (0, 0))],
        out_specs=[pl.BlockSpec((tm, w), lambda i: (i, 0))] * 2,
        out_shape=[jax.ShapeDtypeStruct((n, w), F32)] * 2,
        compiler_params=_cparams(("arbitrary",)),
    )(pos_col, invf)


def _ret_consts(c):
    hh = np.arange(N_HEADS, dtype=np.float64)
    log_gamma = np.log(1.0 - 2.0 ** (-5.0 - hh))
    pos = np.arange(c, dtype=np.float64)
    diff = pos[:, None] - pos[None, :]
    dm = np.where(diff >= 0, np.exp(np.where(diff >= 0, diff, 0.0)[None] * log_gamma[:, None, None]), 0.0)
    col_head = np.arange(N_HEADS * RET_DK) // RET_DK
    qs = np.exp((pos + 1.0)[:, None] * log_gamma[col_head][None, :])
    ks = np.exp((c - 1.0 - pos)[:, None] * log_gamma[col_head][None, :])
    gc = np.exp(c * log_gamma[col_head])[None, :]
    return (jnp.asarray(dm, F32), jnp.asarray(qs, F32), jnp.asarray(ks, F32), jnp.asarray(gc, F32))


def _ret_kernel(q_ref, k_ref, qp_ref, kp_ref, v_ref, g_ref, cos_ref, sin_ref, dm_ref, qs_ref, ks_ref,
                gc_ref, o_ref, st_ref):
    @pl.when(pl.program_id(1) == 0)
    def _():
        st_ref[...] = jnp.zeros_like(st_ref)

    cs = cos_ref[...]
    sn = sin_ref[...]
    q = q_ref[...] * cs + qp_ref[...] * sn
    k = (k_ref[...] * cs + kp_ref[...] * sn) * (RET_DK ** -0.5)
    qb = q.astype(BF16)
    kb = k.astype(BF16)
    q_dec = (q * qs_ref[...]).astype(BF16)
    k_dec = (k * ks_ref[...]).astype(BF16)
    gc = gc_ref[...]
    for h in range(N_HEADS):
        sk = slice(h * RET_DK, (h + 1) * RET_DK)
        sv = slice(h * DV, (h + 1) * DV)
        s = _dot_nt(qb[:, sk], kb[:, sk]) * dm_ref[h]
        vh = v_ref[:, sv].astype(BF16)
        st = st_ref[h]
        o = _dot(s.astype(BF16), vh) + _dot_nt(q_dec[:, sk], st.astype(BF16))
        st_ref[h] = st * gc[:, sk] + _dot_tn(vh, k_dec[:, sk])
        ms = jnp.mean(o * o, axis=-1, keepdims=True)
        gh = g_ref[:, sv]
        o_ref[:, sv] = (o * lax.rsqrt(ms + NORM_EPS) * (gh * _sigmoid(gh))).astype(o_ref.dtype)


def _ret_branch(hcat, seq, cos_t, sin_t, consts, c):
    n = hcat.shape[0]
    grid, nb = _seq_grid(n, seq, c)
    dm, qs, ks, gc = consts
    tab = pl.BlockSpec((c, 256), lambda b, j: (b * nb + j, 0))
    return pl.pallas_call(
        _ret_kernel,
        grid=grid,
        in_specs=[_col_spec(c, 256, "rq", nb), _col_spec(c, 256, "rk", nb),
                  _col_spec(c, 256, "rqp", nb), _col_spec(c, 256, "rkp", nb),
                  _col_spec(c, 512, "rv", nb), _col_spec(c, 512, "rg", nb), tab, tab,
                  _const_spec(dm.shape), _const_spec(qs.shape), _const_spec(ks.shape), _row_spec(256)],
        out_specs=pl.BlockSpec((c, BRANCH_WIDTH), lambda b, j: (b * nb + j, 0)),
        out_shape=jax.ShapeDtypeStruct((n, BRANCH_WIDTH), BF16),
        scratch_shapes=[pltpu.VMEM((N_HEADS, DV, RET_DK), F32)],
        compiler_params=_cparams(("arbitrary", "arbitrary")),
    )(hcat, hcat, hcat, hcat, hcat, hcat, cos_t, sin_t, dm, qs, ks, gc)


def _lru_kernel(x_ref, y_ref, cw_ref, cb_ref, wa_ref, ba_ref, wx_ref, bx_ref, lam_ref, o_ref,
                halo_ref, h_ref, *, c):
    @pl.when(pl.program_id(1) == 0)
    def _():
        halo_ref[...] = jnp.zeros_like(halo_ref)
        h_ref[...] = jnp.zeros_like(h_ref)

    x = x_ref[...]
    x_ext = jnp.concatenate([halo_ref[...], x], axis=0)
    cw = cw_ref[...]
    xc = cw[CONV_WIDTH - 1:CONV_WIDTH, :] * x
    for w in range(CONV_WIDTH - 1):
        shift = CONV_WIDTH - 1 - w
        xc = xc + cw[w:w + 1, :] * pltpu.roll(x_ext, shift, 0)[SUBLANES:, :]
    xc = xc + cb_ref[...]
    halo_ref[...] = x[c - SUBLANES:, :]

    blk = BRANCH_WIDTH // 4
    ra = []
    rx = []
    for g in range(4):
        xg = xc[:, g * blk:(g + 1) * blk].astype(BF16)
        ra.append(_dot(xg, wa_ref[g]))
        rx.append(_dot(xg, wx_ref[g]))
    r = _sigmoid(jnp.concatenate(ra, axis=1) + ba_ref[...])
    ig = _sigmoid(jnp.concatenate(rx, axis=1) + bx_ref[...])
    log_a = -LRU_C * r * _softplus(-lam_ref[...])
    a = jnp.exp(log_a)
    u = jnp.sqrt(-jnp.tanh(log_a) * (a * a + 1.0)) * (ig * xc)

    rows = lax.broadcasted_iota(jnp.int32, (c, 1), 0)
    s = 1
    while s < c:
        keep = rows >= s
        a_sh = jnp.where(keep, pltpu.roll(a, s, 0), 1.0)
        u_sh = jnp.where(keep, pltpu.roll(u, s, 0), 0.0)
        u = a * u_sh + u
        a = a * a_sh
        s *= 2
    hs = u + a * h_ref[0:1, :]
    h_ref[...] = jnp.broadcast_to(hs[c - 1:c, :], h_ref.shape)
    y = y_ref[...]
    gelu = 0.5 * y * (1.0 + jnp.tanh(math.sqrt(2.0 / math.pi) * (y + 0.044715 * (y * y * y))))
    o_ref[...] = (hs * gelu).astype(o_ref.dtype)


def _lru_branch(hcat, seq, cw, cb, wa, ba, wx, bx, lam):
    n = hcat.shape[0]
    c = _tile(seq, LRU_BLOCK)
    grid, nb = _seq_grid(n, seq, c)
    return pl.pallas_call(
        functools.partial(_lru_kernel, c=c),
        grid=grid,
        in_specs=[_col_spec(c, 512, "lx", nb), _col_spec(c, 512, "ly", nb),
                  _const_spec(cw.shape), _row_spec(512), _const_spec(wa.shape), _row_spec(512),
                  _const_spec(wx.shape), _row_spec(512), _row_spec(512)],
        out_specs=pl.BlockSpec((c, BRANCH_WIDTH), lambda b, j: (b * nb + j, 0)),
        out_shape=jax.ShapeDtypeStruct((n, BRANCH_WIDTH), BF16),
        scratch_shapes=[pltpu.VMEM((SUBLANES, BRANCH_WIDTH), F32), pltpu.VMEM((SUBLANES, BRANCH_WIDTH), F32)],
        compiler_params=_cparams(("arbitrary", "arbitrary")),
    )(hcat, hcat, cw, cb, wa, ba, wx, bx, lam)


def _merge_kernel(x_ref, o0_ref, o1_ref, o2_ref, o3_ref, wg_ref, wu_ref, b_ref, out_ref):
    x = x_ref[...]
    acc = None
    for br, o_ref in enumerate((o0_ref, o1_ref, o2_ref, o3_ref)):
        gate = _sigmoid(_dot(x, wg_ref[br]) + b_ref[br])
        term = gate * _dot(o_ref[...], wu_ref[br])
        acc = term if acc is None else acc + term
    out_ref[...] = acc.astype(out_ref.dtype)


def _merge(xb, outs, wg_all, wu_all, b_all, layer):
    n = xb.shape[0]
    tm = _tile(n, 1024)
    tn = 512
    ospec = pl.BlockSpec((tm, BRANCH_WIDTH), lambda i, j: (i, 0))
    return pl.pallas_call(
        _merge_kernel,
        grid=(n // tm, D_MODEL // tn),
        in_specs=[pl.BlockSpec((tm, D_MODEL), lambda i, j: (i, 0)), ospec, ospec, ospec, ospec,
                  pl.BlockSpec((None, 4, D_MODEL, tn), lambda i, j: (layer, 0, 0, j)),
                  pl.BlockSpec((None, 4, BRANCH_WIDTH, tn), lambda i, j: (layer, 0, 0, j)),
                  pl.BlockSpec((None, 4, 1, tn), lambda i, j: (layer, 0, 0, j))],
        out_specs=pl.BlockSpec((tm, tn), lambda i, j: (i, j)),
        out_shape=jax.ShapeDtypeStruct((n, D_MODEL), BF16),
        compiler_params=_cparams(("arbitrary", "arbitrary")),
    )(xb, *outs, wg_all, wu_all, b_all)


def _to_rows(ref, val):
    for j in range(ROW_CHUNKS):
        ref[:, j, :] = val[:, j * LANES:(j + 1) * LANES]


def _from_rows(ref):
    return jnp.concatenate([ref[:, j, :] for j in range(ROW_CHUNKS)], axis=1)


def _outln_kernel(m_ref, w_ref, x_ref, g_ref, b_ref, wr_ref, x1_ref, x1r_ref, lg_ref, *, alpha):
    y = alpha * x_ref[...] + _dot(m_ref[...], w_ref[...])
    x1 = _layer_norm(y, g_ref[...], b_ref[...])
    x1_ref[...] = x1
    _to_rows(x1r_ref, x1)
    lg_ref[...] = _dot_nt(wr_ref[...], x1.astype(BF16))


def _out_proj_ln(merged, w_all, x, g_all, b_all, wr_t, layer, alpha):
    n = x.shape[0]
    tm = _tile(n, 512)
    row = pl.BlockSpec((tm, D_MODEL), lambda i: (i, 0))
    vec = pl.BlockSpec((None, 1, D_MODEL), lambda i: (layer, 0, 0))
    return pl.pallas_call(
        functools.partial(_outln_kernel, alpha=alpha),
        grid=(n // tm,),
        in_specs=[row, pl.BlockSpec((None, D_MODEL, D_MODEL), lambda i: (layer, 0, 0)), row, vec, vec,
                  pl.BlockSpec((N_EXPERTS, D_MODEL), lambda i: (0, 0))],
        out_specs=[row, pl.BlockSpec((tm, ROW_CHUNKS, LANES), lambda i: (i, 0, 0)),
                   pl.BlockSpec((N_EXPERTS, tm), lambda i: (0, i))],
        out_shape=[jax.ShapeDtypeStruct((n, D_MODEL), F32), jax.ShapeDtypeStruct((n, ROW_CHUNKS, LANES), F32),
                   jax.ShapeDtypeStruct((N_EXPERTS, n), F32)],
        compiler_params=_cparams(("arbitrary",)),
    )(merged, w_all, x, g_all, b_all, wr_t)


def _route_kernel(lg_ref, bias_ref, idx_ref, wt_ref):
    s = _sigmoid(lg_ref[...])
    sel = s + bias_ref[...]
    tn = s.shape[1]
    rows = lax.broadcasted_iota(jnp.int32, (EXPERTS_PER_GROUP, tn), 0)
    neg = jnp.float32(-jnp.inf)

    def first_max(blk):
        m = jnp.max(blk, axis=0, keepdims=True)
        i = jnp.min(jnp.where(blk == m, rows, EXPERTS_PER_GROUP), axis=0, keepdims=True)
        return m, i

    best = None
    for g in range(N_GROUPS):
        sl = slice(g * EXPERTS_PER_GROUP, (g + 1) * EXPERTS_PER_GROUP)
        blk = sel[sl]
        sg = s[sl]
        m1, i1 = first_max(blk)
        m2, i2 = first_max(jnp.where(rows == i1, neg, blk))
        w1 = jnp.sum(jnp.where(rows == i1, sg, 0.0), axis=0, keepdims=True)
        w2 = jnp.sum(jnp.where(rows == i2, sg, 0.0), axis=0, keepdims=True)
        cand = (m1 + m2, i1 + g * EXPERTS_PER_GROUP, i2 + g * EXPERTS_PER_GROUP, w1, w2)
        if best is None:
            best = cand
        else:
            take = cand[0] > best[0]
            best = tuple(jnp.where(take, cn, bs) for cn, bs in zip(cand, best))
    _, e1, e2, w1, w2 = best
    tot = w1 + w2
    idx_ref[0:1, :] = e1
    idx_ref[1:2, :] = e2
    wt_ref[0:1, :] = w1 / tot
    wt_ref[1:2, :] = w2 / tot


def _route(logits_t, bias_col):
    n = logits_t.shape[1]
    tn = _tile(n, 2048)
    return pl.pallas_call(
        _route_kernel,
        grid=(n // tn,),
        in_specs=[pl.BlockSpec((N_EXPERTS, tn), lambda i: (0, i)),
                  pl.BlockSpec((N_EXPERTS, 1), lambda i: (0, 0))],
        out_specs=[pl.BlockSpec((2, tn), lambda i: (0, i))] * 2,
        out_shape=[jax.ShapeDtypeStruct((2, n), jnp.int32), jax.ShapeDtypeStruct((2, n), F32)],
        compiler_params=_cparams(("arbitrary",)),
    )(logits_t, bias_col)


def _rank_kernel(e_ref, u_ref, rank_ref, cnt_ref, carry_ref):
    @pl.when(pl.program_id(0) == 0)
    def _():
        carry_ref[...] = jnp.zeros_like(carry_ref)

    e = e_ref[...]
    tn = e.shape[1]
    rows = lax.broadcasted_iota(jnp.int32, (N_EXPERTS, tn), 0)
    onehot = jnp.where(rows == e, 1.0, 0.0)
    before = _dot(onehot.astype(BF16), u_ref[...])
    carry = carry_ref[...]
    rank_ref[...] = jnp.sum(onehot * (before + carry), axis=0, keepdims=True).astype(jnp.int32)
    carry = carry + jnp.sum(onehot, axis=1, keepdims=True)
    carry_ref[...] = carry
    cnt_ref[...] = carry


def _rank(e_row):
    m = e_row.shape[1]
    tn = _tile(m, RANK_TILE)
    upper = jnp.asarray(np.triu(np.ones((tn, tn), np.float32), 1), BF16)
    return pl.pallas_call(
        _rank_kernel,
        grid=(m // tn,),
        in_specs=[pl.BlockSpec((1, tn), lambda i: (0, i)), pl.BlockSpec((tn, tn), lambda i: (0, 0))],
        out_specs=[pl.BlockSpec((1, tn), lambda i: (0, i)), pl.BlockSpec((N_EXPERTS, 1), lambda i: (0, 0))],
        out_shape=[jax.ShapeDtypeStruct((1, m), jnp.int32), jax.ShapeDtypeStruct((N_EXPERTS, 1), F32)],
        scratch_shapes=[pltpu.VMEM((N_EXPERTS, 1), F32)],
        compiler_params=_cparams(("arbitrary",)),
    )(e_row, upper)


def _dispatch_plan(idx_t, n):
    m = 2 * n
    bm = MOE_BLOCK
    e_row = idx_t.reshape(1, m)
    rank, cnt = _rank(e_row)
    counts = cnt.reshape(N_EXPERTS).astype(jnp.int32)
    padded = (counts + bm - 1) // bm * bm
    pad_end = jnp.cumsum(padded)
    pad_start = pad_end - padded
    experts = jnp.arange(N_EXPERTS, dtype=jnp.int32)
    start_of = jnp.sum(jnp.where(e_row == experts[:, None], pad_start[:, None], 0), axis=0)
    dest = (rank.reshape(m) + start_of).astype(jnp.int32)
    n_rows = (m + N_EXPERTS * (bm - 1) + bm - 1) // bm * bm
    n_blocks = n_rows // bm
    starts = jnp.arange(n_blocks, dtype=jnp.int32) * bm
    block_e = jnp.minimum(jnp.sum((pad_end[None, :] <= starts[:, None]).astype(jnp.int32), axis=1),
                          N_EXPERTS - 1).astype(jnp.int32)
    n_used = (pad_end[-1] // bm).astype(jnp.int32).reshape(1)
    return dest, block_e, n_used, n_rows


def _scatter_kernel(dest_ref, x_ref, init_ref, out_ref, sem, *, tm, n):
    del init_ref
    base = pl.program_id(0) * tm

    def row_copy(r, k):
        return pltpu.make_async_copy(x_ref.at[r], out_ref.at[dest_ref[k * n + base + r]], sem)

    def start(r, carry):
        row_copy(r, 0).start()
        row_copy(r, 1).start()
        return carry

    def wait(r, carry):
        row_copy(r, 0).wait()
        row_copy(r, 1).wait()
        return carry

    lax.fori_loop(0, tm, start, 0)
    lax.fori_loop(0, tm, wait, 0)


def _scatter_rows(x_rows, dest, n_rows):
    n = x_rows.shape[0]
    tm = _tile(n, SCATTER_TILE)
    init = jnp.zeros((n_rows, ROW_CHUNKS, LANES), x_rows.dtype)
    return pl.pallas_call(
        functools.partial(_scatter_kernel, tm=tm, n=n),
        grid_spec=pltpu.PrefetchScalarGridSpec(
            num_scalar_prefetch=1,
            grid=(n // tm,),
            in_specs=[pl.BlockSpec((tm, ROW_CHUNKS, LANES), lambda i, d: (i, 0, 0)),
                      pl.BlockSpec(memory_space=pl.ANY)],
            out_specs=pl.BlockSpec(memory_space=pl.ANY),
            scratch_shapes=[pltpu.SemaphoreType.DMA],
        ),
        out_shape=jax.ShapeDtypeStruct((n_rows, ROW_CHUNKS, LANES), x_rows.dtype),
        input_output_aliases={2: 0},
        compiler_params=_cparams(("arbitrary",)),
    )(dest, x_rows, init)


def _moe_kernel(be_ref, nb_ref, x_ref, wgt_ref, wut_ref, wd_ref, y_ref):
    i = pl.program_id(0)

    @pl.when(i < nb_ref[0])
    def _():
        x = _from_rows(x_ref).astype(BF16)
        kc = D_MODEL // 4
        g = None
        u = None
        for j in range(4):
            xs = x[:, j * kc:(j + 1) * kc]
            gj = _dot_nt(xs, wgt_ref[:, j * kc:(j + 1) * kc].astype(BF16))
            uj = _dot_nt(xs, wut_ref[:, j * kc:(j + 1) * kc].astype(BF16))
            g = gj if g is None else g + gj
            u = uj if u is None else u + uj
        hid = ((g * _sigmoid(g)) * u).astype(BF16)
        _to_rows(y_ref, _dot(hid, wd_ref[...].astype(BF16)))

    @pl.when(i >= nb_ref[0])
    def _():
        y_ref[...] = jnp.zeros_like(y_ref)


def _moe_ffn(xs, block_e, n_used, wgt_all, wut_all, wd_all, layer):
    n_rows = xs.shape[0]
    bm = MOE_BLOCK
    rows = pl.BlockSpec((bm, ROW_CHUNKS, LANES), lambda i, be, nb: (i, 0, 0))
    wspec = pl.BlockSpec((None, None, D_EXPERT, D_MODEL), lambda i, be, nb: (layer, be[i], 0, 0))
    return pl.pallas_call(
        _moe_kernel,
        grid_spec=pltpu.PrefetchScalarGridSpec(
            num_scalar_prefetch=2,
            grid=(n_rows // bm,),
            in_specs=[rows, wspec, wspec, wspec],
            out_specs=rows,
        ),
        out_shape=jax.ShapeDtypeStruct((n_rows, ROW_CHUNKS, LANES), F32),
        compiler_params=_cparams(("arbitrary",)),
    )(block_e, n_used, xs, wgt_all, wut_all, wd_all)


def _combine_kernel(dest_ref, x_ref, w_ref, yb_ref, g_ref, b_ref, o_ref, ob_ref, buf_ref, sem, *, tm, n, alpha):
    base = pl.program_id(0) * tm

    def row_copy(r, k):
        return pltpu.make_async_copy(yb_ref.at[dest_ref[k * n + base + r]], buf_ref.at[k, r], sem)

    def start(r, carry):
        row_copy(r, 0).start()
        row_copy(r, 1).start()
        return carry

    def wait(r, carry):
        row_copy(r, 0).wait()
        row_copy(r, 1).wait()
        return carry

    lax.fori_loop(0, tm, start, 0)
    lax.fori_loop(0, tm, wait, 0)
    w = w_ref[...]
    y = w[:, 0:1] * _from_rows(buf_ref.at[0]) + w[:, 1:2] * _from_rows(buf_ref.at[1])
    x2 = _layer_norm(alpha * x_ref[...] + y, g_ref[...], b_ref[...])
    o_ref[...] = x2
    ob_ref[...] = x2.astype(BF16)


def _combine_ln(x1, w_col, yb, dest, g_all, b_all, layer, alpha):
    n = x1.shape[0]
    tm = _tile(n, COMBINE_TILE)
    row = pl.BlockSpec((tm, D_MODEL), lambda i, d: (i, 0))
    vec = pl.BlockSpec((None, 1, D_MODEL), lambda i, d: (layer, 0, 0))
    return pl.pallas_call(
        functools.partial(_combine_kernel, tm=tm, n=n, alpha=alpha),
        grid_spec=pltpu.PrefetchScalarGridSpec(
            num_scalar_prefetch=1,
            grid=(n // tm,),
            in_specs=[row, pl.BlockSpec((tm, 2), lambda i, d: (i, 0)), pl.BlockSpec(memory_space=pl.ANY), vec, vec],
            out_specs=[row, row],
            scratch_shapes=[pltpu.VMEM((2, tm, ROW_CHUNKS, LANES), F32), pltpu.SemaphoreType.DMA],
        ),
        out_shape=[jax.ShapeDtypeStruct((n, D_MODEL), F32), jax.ShapeDtypeStruct((n, D_MODEL), BF16)],
        compiler_params=_cparams(("arbitrary",)),
    )(dest, x1, w_col, yb, g_all, b_all)


def _relayout_w_in(w_in):
    def seg(name):
        a, b = _ORIG[name]
        return w_in[..., a:b]

    def partner(name):
        a0 = _ORIG[name][0]
        half = RET_DK // 2
        parts = []
        for h in range(N_HEADS):
            b = a0 + RET_DK * h
            parts += [-w_in[..., b + half:b + RET_DK], w_in[..., b:b + half]]
        return parts

    lead = w_in.shape[:-1]
    cols = ([seg(nm) for nm in _SEG512] + [seg("gq"), seg("gk"), seg("rq"), seg("rk")]
            + partner("rq") + partner("rk")
            + [seg("glr"), jnp.zeros(lead + (LANES - GLA_RANK + _H_PAD,), w_in.dtype)])
    return jnp.concatenate(cols, axis=-1).astype(BF16)


def kernel(x, positions, w_in, w_gla_gate2, b_gla_gate2, gla_norm, hgrn_lb, hgrn_norm, conv_w, conv_b,
           w_lru_a, b_lru_a, w_lru_x, b_lru_x, lru_lambda, w_branch_up, w_merge_gate, b_merge_gate, w_out,
           ln1_g, ln1_b, w_router, router_bias, w_exp_gate, w_exp_up, w_exp_down, ln2_g, ln2_b):
    bsz, seq, d = x.shape
    depth = w_in.shape[0]
    n = bsz * seq
    alpha = (2 * depth) ** 0.25

    w_in_b = _relayout_w_in(w_in)
    w2_b = jnp.pad(w_gla_gate2, ((0, 0), (0, LANES - GLA_RANK), (0, 0))).astype(BF16)
    b2 = b_gla_gate2.reshape(depth, 1, -1)
    gla_nw = jnp.tile(gla_norm, (1, N_HEADS)).reshape(depth, 1, -1)
    hgrn_nw = jnp.tile(hgrn_norm, (1, N_HEADS)).reshape(depth, 1, -1)
    lb_cum = jnp.cumsum(jax.nn.softmax(hgrn_lb.astype(F32), axis=0), axis=0)
    lower = lb_cum - lb_cum[:1]
    log_lb = jnp.log(lower).reshape(depth, 1, -1)
    log_1m_lb = jnp.log1p(-lower).reshape(depth, 1, -1)
    one_m_lb = (1.0 - lower).reshape(depth, 1, -1)
    wa_b = w_lru_a.astype(BF16)
    wx_b = w_lru_x.astype(BF16)
    wg_b = w_merge_gate.astype(BF16)
    wu_b = w_branch_up.astype(BF16)
    bmg = b_merge_gate.reshape(depth, 4, 1, d)
    w_out_b = w_out.astype(BF16)
    wr_t = w_router.T.astype(BF16)
    bias_col = router_bias.reshape(N_EXPERTS, 1).astype(F32)
    ln1g = ln1_g.reshape(depth, 1, d)
    ln1b = ln1_b.reshape(depth, 1, d)
    ln2g = ln2_g.reshape(depth, 1, d)
    ln2b = ln2_b.reshape(depth, 1, d)
    wgt = jnp.swapaxes(w_exp_gate, 2, 3)
    wut = jnp.swapaxes(w_exp_up, 2, 3)

    half = RET_DK // 2
    inv_freq = ROPE_BASE ** (-jnp.arange(half, dtype=F32) / half)
    invf = jnp.tile(inv_freq, 2 * N_HEADS).reshape(1, N_HEADS * RET_DK)
    cos_t, sin_t = _rope_tables(positions.reshape(n, 1), invf)

    cum, lv = _gated_consts(_tile(seq, GATED_BLOCK))
    ret_c = _tile(seq, RET_BLOCK)
    ret_consts = _ret_consts(ret_c)

    x2d = x.reshape(n, d)
    xb = x2d.astype(BF16)
    for l in range(depth):
        hcat = _in_proj(xb, w_in_b, l)
        o_gla = _gla_branch(hcat, seq, w2_b[l], b2[l], gla_nw[l], cum, lv)
        o_hgrn = _hgrn_branch(hcat, seq, log_lb[l], log_1m_lb[l], one_m_lb[l], hgrn_nw[l], cum, lv)
        o_ret = _ret_branch(hcat, seq, cos_t, sin_t, ret_consts, ret_c)
        o_lru = _lru_branch(hcat, seq, conv_w[l], conv_b[l].reshape(1, -1), wa_b[l], b_lru_a[l].reshape(1, -1),
                            wx_b[l], b_lru_x[l].reshape(1, -1), lru_lambda[l].reshape(1, -1))
        merged = _merge(xb, (o_gla, o_hgrn, o_ret, o_lru), wg_b, wu_b, bmg, l)
        x1, x1_rows, logits_t = _out_proj_ln(merged, w_out_b, x2d, ln1g, ln1b, wr_t, l, alpha)
        idx_t, wt_t = _route(logits_t, bias_col)
        dest, block_e, n_used, n_rows = _dispatch_plan(idx_t, n)
        xs = _scatter_rows(x1_rows, dest, n_rows)
        yb = _moe_ffn(xs, block_e, n_used, wgt, wut, w_exp_down, l)
        x2d, xb = _combine_ln(x1, wt_t.T, yb, dest, ln2g, ln2b, l, alpha)
    return x2d.reshape(bsz, seq, d)
```

```python
import functools
import math

import numpy as np
import jax
import jax.numpy as jnp
from jax import lax
from jax.experimental import pallas as pl
from jax.experimental.pallas import tpu as pltpu

F32 = jnp.float32
BF16 = jnp.bfloat16

D_MODEL = 2048
N_HEADS = 4
DV = 128
BRANCH_WIDTH = 512
GLA_DK = 64
GLA_RANK = 16
GLA_GATE_NORM = 16.0
HGRN_DK = 128
RET_DK = 64
ROPE_BASE = 10000.0
CONV_WIDTH = 4
LRU_C = 8.0
N_EXPERTS = 32
N_GROUPS = 4
EXPERTS_PER_GROUP = 8
D_EXPERT = 704
NORM_EPS = 1e-5

LANES = 128
SUBLANES = 8
ROW_CHUNKS = D_MODEL // LANES
ROW_PITCH = ROW_CHUNKS + 1
VMEM_LIMIT = 56 * 1024 * 1024

_ORIG_SIZES = (("gq", 256), ("gk", 256), ("gv", 512), ("gg", 512), ("glr", 16),
               ("hq", 512), ("hf", 512), ("hi", 512), ("hg", 512),
               ("rq", 256), ("rk", 256), ("rv", 512), ("rg", 512), ("lx", 512), ("ly", 512))
_ORIG = {}
_o = 0
for _n, _w in _ORIG_SIZES:
    _ORIG[_n] = (_o, _o + _w)
    _o += _w

_SEG512 = ("gv", "gg", "hq", "hf", "hi", "hg", "rv", "rg", "lx", "ly")
_SEG256 = ("gq", "gk", "rq", "rk", "rqp", "rkp")
_OFF = {}
_o = 0
for _n in _SEG512:
    _OFF[_n] = _o
    _o += 512
for _n in _SEG256:
    _OFF[_n] = _o
    _o += 256
_OFF["glr"] = _o
_o += LANES
H_COLS = 6912
_H_PAD = H_COLS - _o

GATED_BLOCK = 128
RET_BLOCK = 256
LRU_BLOCK = 256
MOE_BLOCK = 256
SCATTER_TILE = 512
COMBINE_TILE = 256
RANK_TILE = 1024


def _tile(n, pref):
    return pref if n % pref == 0 else n


def _cparams(sem):
    return pltpu.CompilerParams(dimension_semantics=sem, vmem_limit_bytes=VMEM_LIMIT)


def _dot(a, b):
    return jnp.dot(a, b, preferred_element_type=F32)


def _dot_nt(a, b):
    return lax.dot_general(a, b, (((1,), (1,)), ((), ())), preferred_element_type=F32)


def _dot_tn(a, b):
    return lax.dot_general(a, b, (((0,), (0,)), ((), ())), preferred_element_type=F32)


def _sigmoid(x):
    return 1.0 / (1.0 + jnp.exp(-x))


def _log_sigmoid(x):
    return jnp.minimum(x, 0.0) - jnp.log(1.0 + jnp.exp(-jnp.abs(x)))


def _softplus(x):
    return jnp.maximum(x, 0.0) + jnp.log(1.0 + jnp.exp(-jnp.abs(x)))


def _layer_norm(y, g, b):
    mu = jnp.mean(y, axis=-1, keepdims=True)
    yc = y - mu
    var = jnp.mean(yc * yc, axis=-1, keepdims=True)
    return yc * lax.rsqrt(var + NORM_EPS) * g + b


def _mm_kernel(x_ref, w_ref, o_ref):
    o_ref[...] = _dot(x_ref[...], w_ref[...]).astype(o_ref.dtype)


def _in_proj(xb, w_all, layer):
    n, k = xb.shape
    cols = w_all.shape[-1]
    tm = _tile(n, 512)
    tn = 2304
    return pl.pallas_call(
        _mm_kernel,
        grid=(cols // tn, n // tm),
        in_specs=[pl.BlockSpec((tm, k), lambda j, i: (i, 0)),
                  pl.BlockSpec((None, k, tn), lambda j, i: (layer, 0, j))],
        out_specs=pl.BlockSpec((tm, tn), lambda j, i: (i, j)),
        out_shape=jax.ShapeDtypeStruct((n, cols), F32),
        compiler_params=_cparams(("arbitrary", "arbitrary")),
    )(xb, w_all)


def _gated_consts(c):
    n16 = c // 16
    i = np.arange(c)
    g = i // 16
    mats = []
    hs = []
    h = 1
    while h <= n16:
        hs.append(h)
        h *= 2
    for h in hs:
        grp = g // h
        same = grp[:, None] == grp[None, :]
        mats.append(same & (i[None, :] <= i[:, None]))
        mats.append(same & (i[None, :] > i[:, None]))
    cum = np.concatenate(mats, 0).astype(np.float32)
    lv = -np.ones((c, c), np.int32)
    lv[(g[:, None] == g[None, :]) & (i[None, :] <= i[:, None])] = 0
    gi = g[:, None]
    gj = g[None, :]
    for li, h in enumerate(hs[:-1]):
        m = (gi // (2 * h) == gj // (2 * h)) & ((gi // h) % 2 == 1) & ((gj // h) % 2 == 0)
        lv[m] = li + 1
    return jnp.asarray(cum, BF16), jnp.asarray(lv)


def _exact_select_dot(sel, x):
    hi = x.astype(BF16)
    r1 = x - hi.astype(F32)
    mid = r1.astype(BF16)
    lo = (r1 - mid.astype(F32)).astype(BF16)
    return _dot(sel, hi) + _dot(sel, mid) + _dot(sel, lo)


def _gated_core(q, k, la, v_ref, g_ref, nw_ref, cum_ref, lv_ref, o_ref, st_ref, *, c, dk):
    ex = _exact_select_dot(cum_ref[...], la)
    nl = ex.shape[0] // (2 * c)
    c_t = [ex[(2 * l) * c:(2 * l + 1) * c] for l in range(nl)]
    c_e = [ex[(2 * l + 1) * c:(2 * l + 2) * c] for l in range(nl)]
    qd = [(q * jnp.exp(c_t[l])).astype(BF16) for l in range(nl)]
    k_diag = (k * jnp.exp(-c_t[0])).astype(BF16)
    kd = [(k * jnp.exp(c_e[l])).astype(BF16) for l in range(nl)]
    dec = jnp.exp(c_t[nl - 1][c - 1:c, :])
    lv = lv_ref[...]
    for h in range(N_HEADS):
        sk = slice(h * dk, (h + 1) * dk)
        sv = slice(h * DV, (h + 1) * DV)
        p = jnp.where(lv == 0, _dot_nt(qd[0][:, sk], k_diag[:, sk]), 0.0)
        for l in range(nl - 1):
            p = p + jnp.where(lv == l + 1, _dot_nt(qd[l][:, sk], kd[l][:, sk]), 0.0)
        vh = v_ref[:, sv].astype(BF16)
        st = st_ref[h]
        o = _dot(p.astype(BF16), vh) + _dot_nt(qd[nl - 1][:, sk], st.astype(BF16))
        st_ref[h] = st * dec[:, sk] + _dot_tn(vh, kd[nl - 1][:, sk])
        ms = jnp.mean(o * o, axis=-1, keepdims=True)
        gh = g_ref[:, sv]
        o_ref[:, sv] = (o * lax.rsqrt(ms + NORM_EPS) * nw_ref[:, sv] * (gh * _sigmoid(gh))).astype(o_ref.dtype)


def _gla_kernel(q_ref, k_ref, v_ref, g_ref, lr_ref, w2_ref, b2_ref, nw_ref, cum_ref, lv_ref,
                o_ref, st_ref, *, c):
    @pl.when(pl.program_id(1) == 0)
    def _():
        st_ref[...] = jnp.zeros_like(st_ref)

    q = q_ref[...] * (GLA_DK ** -0.5)
    logits = _dot(lr_ref[...].astype(BF16), w2_ref[...]) + b2_ref[...]
    la = _log_sigmoid(logits) * (1.0 / GLA_GATE_NORM)
    _gated_core(q, k_ref[...], la, v_ref, g_ref, nw_ref, cum_ref, lv_ref, o_ref, st_ref, c=c, dk=GLA_DK)


def _hgrn_kernel(q_ref, f_ref, v_ref, g_ref, llb_ref, l1m_ref, oml_ref, nw_ref, cum_ref, lv_ref,
                 o_ref, st_ref, *, c):
    @pl.when(pl.program_id(1) == 0)
    def _():
        st_ref[...] = jnp.zeros_like(st_ref)

    z = f_ref[...]
    a = llb_ref[...]
    b = l1m_ref[...] + _log_sigmoid(z)
    la = jnp.maximum(a, b) + jnp.log(1.0 + jnp.exp(-jnp.abs(a - b)))
    k = oml_ref[...] * _sigmoid(-z)
    qq = q_ref[...]
    q = qq * _sigmoid(qq)
    _gated_core(q, k, la, v_ref, g_ref, nw_ref, cum_ref, lv_ref, o_ref, st_ref, c=c, dk=HGRN_DK)


def _seq_grid(n, seq, c):
    nb = seq // c
    return (n // seq, nb), nb


def _col_spec(c, width, name, nb):
    cb = _OFF[name] // width
    return pl.BlockSpec((c, width), lambda b, j: (b * nb + j, cb))


def _row_spec(width):
    return pl.BlockSpec((1, width), lambda b, j: (0, 0))


def _const_spec(shape):
    nd = len(shape)
    return pl.BlockSpec(shape, lambda b, j: (0,) * nd)


def _gla_branch(hcat, seq, w2, b2, nw, cum, lv):
    n = hcat.shape[0]
    c = _tile(seq, GATED_BLOCK)
    grid, nb = _seq_grid(n, seq, c)
    return pl.pallas_call(
        functools.partial(_gla_kernel, c=c),
        grid=grid,
        in_specs=[_col_spec(c, 256, "gq", nb), _col_spec(c, 256, "gk", nb),
                  _col_spec(c, 512, "gv", nb), _col_spec(c, 512, "gg", nb),
                  _col_spec(c, LANES, "glr", nb),
                  _const_spec(w2.shape), _row_spec(256), _row_spec(512),
                  _const_spec(cum.shape), _const_spec(lv.shape)],
        out_specs=pl.BlockSpec((c, BRANCH_WIDTH), lambda b, j: (b * nb + j, 0)),
        out_shape=jax.ShapeDtypeStruct((n, BRANCH_WIDTH), BF16),
        scratch_shapes=[pltpu.VMEM((N_HEADS, DV, GLA_DK), F32)],
        compiler_params=_cparams(("arbitrary", "arbitrary")),
    )(hcat, hcat, hcat, hcat, hcat, w2, b2, nw, cum, lv)


def _hgrn_branch(hcat, seq, llb, l1m, oml, nw, cum, lv):
    n = hcat.shape[0]
    c = _tile(seq, GATED_BLOCK)
    grid, nb = _seq_grid(n, seq, c)
    return pl.pallas_call(
        functools.partial(_hgrn_kernel, c=c),
        grid=grid,
        in_specs=[_col_spec(c, 512, "hq", nb), _col_spec(c, 512, "hf", nb),
                  _col_spec(c, 512, "hi", nb), _col_spec(c, 512, "hg", nb),
                  _row_spec(512), _row_spec(512), _row_spec(512), _row_spec(512),
                  _const_spec(cum.shape), _const_spec(lv.shape)],
        out_specs=pl.BlockSpec((c, BRANCH_WIDTH), lambda b, j: (b * nb + j, 0)),
        out_shape=jax.ShapeDtypeStruct((n, BRANCH_WIDTH), BF16),
        scratch_shapes=[pltpu.VMEM((N_HEADS, DV, HGRN_DK), F32)],
        compiler_params=_cparams(("arbitrary", "arbitrary")),
    )(hcat, hcat, hcat, hcat, llb, l1m, oml, nw, cum, lv)


def _rope_kernel(pos_ref, invf_ref, cos_ref, sin_ref):
    ang = pos_ref[...].astype(F32) * invf_ref[...]
    cos_ref[...] = jnp.cos(ang)
    sin_ref[...] = jnp.sin(ang)


def _rope_tables(pos_col, invf):
    n = pos_col.shape[0]
    tm = _tile(n, 2048)
    w = invf.shape[1]
    return pl.pallas_call(
        _rope_kernel,
        grid=(n // tm,),
        in_specs=[pl.BlockSpec((tm, 1), lambda i: (i, 0)), pl.BlockSpec((1, w), lambda i: (0, 0))],
        out_specs=[pl.BlockSpec((tm, w), lambda i: (i, 0))] * 2,
        out_shape=[jax.ShapeDtypeStruct((n, w), F32)] * 2,
        compiler_params=_cparams(("arbitrary",)),
    )(pos_col, invf)


def _ret_consts(c):
    hh = np.arange(N_HEADS, dtype=np.float64)
    log_gamma = np.log(1.0 - 2.0 ** (-5.0 - hh))
    pos = np.arange(c, dtype=np.float64)
    diff = pos[:, None] - pos[None, :]
    dm = np.where(diff >= 0, np.exp(np.where(diff >= 0, diff, 0.0)[None] * log_gamma[:, None, None]), 0.0)
    col_head = np.arange(N_HEADS * RET_DK) // RET_DK
    qs = np.exp((pos + 1.0)[:, None] * log_gamma[col_head][None, :])
    ks = np.exp((c - 1.0 - pos)[:, None] * log_gamma[col_head][None, :])
    gc = np.exp(c * log_gamma[col_head])[None, :]
    return (jnp.asarray(dm, F32), jnp.asarray(qs, F32), jnp.asarray(ks, F32), jnp.asarray(gc, F32))


def _ret_kernel(q_ref, k_ref, qp_ref, kp_ref, v_ref, g_ref, cos_ref, sin_ref, dm_ref, qs_ref, ks_ref,
                gc_ref, o_ref, st_ref):
    @pl.when(pl.program_id(1) == 0)
    def _():
        st_ref[...] = jnp.zeros_like(st_ref)

    cs = cos_ref[...]
    sn = sin_ref[...]
    q = q_ref[...] * cs + qp_ref[...] * sn
    k = (k_ref[...] * cs + kp_ref[...] * sn) * (RET_DK ** -0.5)
    qb = q.astype(BF16)
    kb = k.astype(BF16)
    q_dec = (q * qs_ref[...]).astype(BF16)
    k_dec = (k * ks_ref[...]).astype(BF16)
    gc = gc_ref[...]
    for h in range(N_HEADS):
        sk = slice(h * RET_DK, (h + 1) * RET_DK)
        sv = slice(h * DV, (h + 1) * DV)
        s = _dot_nt(qb[:, sk], kb[:, sk]) * dm_ref[h]
        vh = v_ref[:, sv].astype(BF16)
        st = st_ref[h]
        o = _dot(s.astype(BF16), vh) + _dot_nt(q_dec[:, sk], st.astype(BF16))
        st_ref[h] = st * gc[:, sk] + _dot_tn(vh, k_dec[:, sk])
        ms = jnp.mean(o * o, axis=-1, keepdims=True)
        gh = g_ref[:, sv]
        o_ref[:, sv] = (o * lax.rsqrt(ms + NORM_EPS) * (gh * _sigmoid(gh))).astype(o_ref.dtype)


def _ret_branch(hcat, seq, cos_t, sin_t, consts, c):
    n = hcat.shape[0]
    grid, nb = _seq_grid(n, seq, c)
    dm, qs, ks, gc = consts
    tab = pl.BlockSpec((c, 256), lambda b, j: (b * nb + j, 0))
    return pl.pallas_call(
        _ret_kernel,
        grid=grid,
        in_specs=[_col_spec(c, 256, "rq", nb), _col_spec(c, 256, "rk", nb),
                  _col_spec(c, 256, "rqp", nb), _col_spec(c, 256, "rkp", nb),
                  _col_spec(c, 512, "rv", nb), _col_spec(c, 512, "rg", nb), tab, tab,
                  _const_spec(dm.shape), _const_spec(qs.shape), _const_spec(ks.shape), _row_spec(256)],
        out_specs=pl.BlockSpec((c, BRANCH_WIDTH), lambda b, j: (b * nb + j, 0)),
        out_shape=jax.ShapeDtypeStruct((n, BRANCH_WIDTH), BF16),
        scratch_shapes=[pltpu.VMEM((N_HEADS, DV, RET_DK), F32)],
        compiler_params=_cparams(("arbitrary", "arbitrary")),
    )(hcat, hcat, hcat, hcat, hcat, hcat, cos_t, sin_t, dm, qs, ks, gc)


def _lru_kernel(x_ref, y_ref, cw_ref, cb_ref, wa_ref, ba_ref, wx_ref, bx_ref, lam_ref, o_ref,
                halo_ref, h_ref, *, c):
    @pl.when(pl.program_id(1) == 0)
    def _():
        halo_ref[...] = jnp.zeros_like(halo_ref)
        h_ref[...] = jnp.zeros_like(h_ref)

    x = x_ref[...]
    x_ext = jnp.concatenate([halo_ref[...], x], axis=0)
    cw = cw_ref[...]
    xc = cw[CONV_WIDTH - 1:CONV_WIDTH, :] * x
    for w in range(CONV_WIDTH - 1):
        shift = CONV_WIDTH - 1 - w
        xc = xc + cw[w:w + 1, :] * pltpu.roll(x_ext, shift, 0)[SUBLANES:, :]
    xc = xc + cb_ref[...]
    halo_ref[...] = x[c - SUBLANES:, :]

    blk = BRANCH_WIDTH // 4
    ra = []
    rx = []
    for g in range(4):
        xg = xc[:, g * blk:(g + 1) * blk].astype(BF16)
        ra.append(_dot(xg, wa_ref[g]))
        rx.append(_dot(xg, wx_ref[g]))
    r = _sigmoid(jnp.concatenate(ra, axis=1) + ba_ref[...])
    ig = _sigmoid(jnp.concatenate(rx, axis=1) + bx_ref[...])
    log_a = -LRU_C * r * _softplus(-lam_ref[...])
    a = jnp.exp(log_a)
    u = jnp.sqrt(-jnp.tanh(log_a) * (a * a + 1.0)) * (ig * xc)

    rows = lax.broadcasted_iota(jnp.int32, (c, 1), 0)
    s = 1
    while s < c:
        keep = rows >= s
        a_sh = jnp.where(keep, pltpu.roll(a, s, 0), 1.0)
        u_sh = jnp.where(keep, pltpu.roll(u, s, 0), 0.0)
        u = a * u_sh + u
        a = a * a_sh
        s *= 2
    hs = u + a * h_ref[0:1, :]
    h_ref[...] = jnp.broadcast_to(hs[c - 1:c, :], h_ref.shape)
    y = y_ref[...]
    gelu = 0.5 * y * (1.0 + jnp.tanh(math.sqrt(2.0 / math.pi) * (y + 0.044715 * (y * y * y))))
    o_ref[...] = (hs * gelu).astype(o_ref.dtype)


def _lru_branch(hcat, seq, cw, cb, wa, ba, wx, bx, lam):
    n = hcat.shape[0]
    c = _tile(seq, LRU_BLOCK)
    grid, nb = _seq_grid(n, seq, c)
    return pl.pallas_call(
        functools.partial(_lru_kernel, c=c),
        grid=grid,
        in_specs=[_col_spec(c, 512, "lx", nb), _col_spec(c, 512, "ly", nb),
                  _const_spec(cw.shape), _row_spec(512), _const_spec(wa.shape), _row_spec(512),
                  _const_spec(wx.shape), _row_spec(512), _row_spec(512)],
        out_specs=pl.BlockSpec((c, BRANCH_WIDTH), lambda b, j: (b * nb + j, 0)),
        out_shape=jax.ShapeDtypeStruct((n, BRANCH_WIDTH), BF16),
        scratch_shapes=[pltpu.VMEM((SUBLANES, BRANCH_WIDTH), F32), pltpu.VMEM((SUBLANES, BRANCH_WIDTH), F32)],
        compiler_params=_cparams(("arbitrary", "arbitrary")),
    )(hcat, hcat, cw, cb, wa, ba, wx, bx, lam)


def _merge_kernel(x_ref, o0_ref, o1_ref, o2_ref, o3_ref, wg_ref, wu_ref, b_ref, out_ref):
    x = x_ref[...]
    acc = None
    for br, o_ref in enumerate((o0_ref, o1_ref, o2_ref, o3_ref)):
        gate = _sigmoid(_dot(x, wg_ref[br]) + b_ref[br])
        term = gate * _dot(o_ref[...], wu_ref[br])
        acc = term if acc is None else acc + term
    out_ref[...] = acc.astype(out_ref.dtype)


def _merge(xb, outs, wg_all, wu_all, b_all, layer):
    n = xb.shape[0]
    tm = _tile(n, 1024)
    tn = 512
    ospec = pl.BlockSpec((tm, BRANCH_WIDTH), lambda i, j: (i, 0))
    return pl.pallas_call(
        _merge_kernel,
        grid=(n // tm, D_MODEL // tn),
        in_specs=[pl.BlockSpec((tm, D_MODEL), lambda i, j: (i, 0)), ospec, ospec, ospec, ospec,
                  pl.BlockSpec((None, 4, D_MODEL, tn), lambda i, j: (layer, 0, 0, j)),
                  pl.BlockSpec((None, 4, BRANCH_WIDTH, tn), lambda i, j: (layer, 0, 0, j)),
                  pl.BlockSpec((None, 4, 1, tn), lambda i, j: (layer, 0, 0, j))],
        out_specs=pl.BlockSpec((tm, tn), lambda i, j: (i, j)),
        out_shape=jax.ShapeDtypeStruct((n, D_MODEL), BF16),
        compiler_params=_cparams(("arbitrary", "arbitrary")),
    )(xb, *outs, wg_all, wu_all, b_all)


def _to_rows(ref, val):
    m = val.shape[0]
    for j in range(ROW_CHUNKS):
        ref[pl.ds(j, m, stride=ROW_PITCH), :] = val[:, j * LANES:(j + 1) * LANES]
    for j in range(ROW_CHUNKS, ROW_PITCH):
        ref[pl.ds(j, m, stride=ROW_PITCH), :] = jnp.zeros((m, LANES), val.dtype)


def _from_rows(ref, m):
    return jnp.concatenate([ref[pl.ds(j, m, stride=ROW_PITCH), :] for j in range(ROW_CHUNKS)], axis=1)


def _outln_kernel(m_ref, w_ref, x_ref, g_ref, b_ref, wr_ref, x1_ref, x1r_ref, lg_ref, *, alpha):
    y = alpha * x_ref[...] + _dot(m_ref[...], w_ref[...])
    x1 = _layer_norm(y, g_ref[...], b_ref[...])
    x1_ref[...] = x1
    _to_rows(x1r_ref, x1)
    lg_ref[...] = _dot_nt(wr_ref[...], x1.astype(BF16))


def _out_proj_ln(merged, w_all, x, g_all, b_all, wr_t, layer, alpha):
    n = x.shape[0]
    tm = _tile(n, 512)
    row = pl.BlockSpec((tm, D_MODEL), lambda i: (i, 0))
    vec = pl.BlockSpec((None, 1, D_MODEL), lambda i: (layer, 0, 0))
    return pl.pallas_call(
        functools.partial(_outln_kernel, alpha=alpha),
        grid=(n // tm,),
        in_specs=[row, pl.BlockSpec((None, D_MODEL, D_MODEL), lambda i: (layer, 0, 0)), row, vec, vec,
                  pl.BlockSpec((N_EXPERTS, D_MODEL), lambda i: (0, 0))],
        out_specs=[row, pl.BlockSpec((tm * ROW_PITCH, LANES), lambda i: (i, 0)),
                   pl.BlockSpec((N_EXPERTS, tm), lambda i: (0, i))],
        out_shape=[jax.ShapeDtypeStruct((n, D_MODEL), F32), jax.ShapeDtypeStruct((n * ROW_PITCH, LANES), F32),
                   jax.ShapeDtypeStruct((N_EXPERTS, n), F32)],
        compiler_params=_cparams(("arbitrary",)),
    )(merged, w_all, x, g_all, b_all, wr_t)


def _route_kernel(lg_ref, bias_ref, idx_ref, wt_ref):
    s = _sigmoid(lg_ref[...])
    sel = s + bias_ref[...]
    tn = s.shape[1]
    rows = lax.broadcasted_iota(jnp.int32, (EXPERTS_PER_GROUP, tn), 0)
    neg = jnp.float32(-jnp.inf)

    def first_max(blk):
        m = jnp.max(blk, axis=0, keepdims=True)
        i = jnp.min(jnp.where(blk == m, rows, EXPERTS_PER_GROUP), axis=0, keepdims=True)
        return m, i

    best = None
    for g in range(N_GROUPS):
        sl = slice(g * EXPERTS_PER_GROUP, (g + 1) * EXPERTS_PER_GROUP)
        blk = sel[sl]
        sg = s[sl]
        m1, i1 = first_max(blk)
        m2, i2 = first_max(jnp.where(rows == i1, neg, blk))
        w1 = jnp.sum(jnp.where(rows == i1, sg, 0.0), axis=0, keepdims=True)
        w2 = jnp.sum(jnp.where(rows == i2, sg, 0.0), axis=0, keepdims=True)
        cand = (m1 + m2, i1 + g * EXPERTS_PER_GROUP, i2 + g * EXPERTS_PER_GROUP, w1, w2)
        if best is None:
            best = cand
        else:
            take = cand[0] > best[0]
            best = tuple(jnp.where(take, cn, bs) for cn, bs in zip(cand, best))
    _, e1, e2, w1, w2 = best
    tot = w1 + w2
    idx_ref[0:1, :] = e1
    idx_ref[1:2, :] = e2
    wt_ref[0:1, :] = w1 / tot
    wt_ref[1:2, :] = w2 / tot


def _route(logits_t, bias_col):
    n = logits_t.shape[1]
    tn = _tile(n, 2048)
    return pl.pallas_call(
        _route_kernel,
        grid=(n // tn,),
        in_specs=[pl.BlockSpec((N_EXPERTS, tn), lambda i: (0, i)),
                  pl.BlockSpec((N_EXPERTS, 1), lambda i: (0, 0))],
        out_specs=[pl.BlockSpec((2, tn), lambda i: (0, i))] * 2,
        out_shape=[jax.ShapeDtypeStruct((2, n), jnp.int32), jax.ShapeDtypeStruct((2, n), F32)],
        compiler_params=_cparams(("arbitrary",)),
    )(logits_t, bias_col)


def _rank_kernel(e_ref, u_ref, rank_ref, cnt_ref, carry_ref):
    @pl.when(pl.program_id(0) == 0)
    def _():
        carry_ref[...] = jnp.zeros_like(carry_ref)

    e = e_ref[...]
    tn = e.shape[1]
    rows = lax.broadcasted_iota(jnp.int32, (N_EXPERTS, tn), 0)
    onehot = jnp.where(rows == e, 1.0, 0.0)
    before = _dot(onehot.astype(BF16), u_ref[...])
    carry = carry_ref[...]
    rank_ref[...] = jnp.sum(onehot * (before + carry), axis=0, keepdims=True).astype(jnp.int32)
    carry = carry + jnp.sum(onehot, axis=1, keepdims=True)
    carry_ref[...] = carry
    cnt_ref[...] = carry


def _rank(e_row):
    m = e_row.shape[1]
    tn = _tile(m, RANK_TILE)
    upper = jnp.asarray(np.triu(np.ones((tn, tn), np.float32), 1), BF16)
    return pl.pallas_call(
        _rank_kernel,
        grid=(m // tn,),
        in_specs=[pl.BlockSpec((1, tn), lambda i: (0, i)), pl.BlockSpec((tn, tn), lambda i: (0, 0))],
        out_specs=[pl.BlockSpec((1, tn), lambda i: (0, i)), pl.BlockSpec((N_EXPERTS, 1), lambda i: (0, 0))],
        out_shape=[jax.ShapeDtypeStruct((1, m), jnp.int32), jax.ShapeDtypeStruct((N_EXPERTS, 1), F32)],
        scratch_shapes=[pltpu.VMEM((N_EXPERTS, 1), F32)],
        compiler_params=_cparams(("arbitrary",)),
    )(e_row, upper)


def _dispatch_plan(idx_t, n):
    m = 2 * n
    bm = MOE_BLOCK
    e_row = idx_t.reshape(1, m)
    rank, cnt = _rank(e_row)
    counts = cnt.reshape(N_EXPERTS).astype(jnp.int32)
    padded = (counts + bm - 1) // bm * bm
    pad_end = jnp.cumsum(padded)
    pad_start = pad_end - padded
    experts = jnp.arange(N_EXPERTS, dtype=jnp.int32)
    start_of = jnp.sum(jnp.where(e_row == experts[:, None], pad_start[:, None], 0), axis=0)
    dest = (rank.reshape(m) + start_of).astype(jnp.int32)
    n_rows = (m + N_EXPERTS * (bm - 1) + bm - 1) // bm * bm
    n_blocks = n_rows // bm
    starts = jnp.arange(n_blocks, dtype=jnp.int32) * bm
    block_e = jnp.minimum(jnp.sum((pad_end[None, :] <= starts[:, None]).astype(jnp.int32), axis=1),
                          N_EXPERTS - 1).astype(jnp.int32)
    n_used = (pad_end[-1] // bm).astype(jnp.int32).reshape(1)
    return dest, block_e, n_used, n_rows


def _scatter_kernel(dest_ref, x_ref, init_ref, out_ref, sem, *, tm, n):
    del init_ref
    base = pl.program_id(0) * tm

    def row_copy(r, k):
        return pltpu.make_async_copy(x_ref.at[pl.ds(r * ROW_PITCH, ROW_CHUNKS)],
                                     out_ref.at[pl.ds(dest_ref[k * n + base + r] * ROW_PITCH, ROW_CHUNKS)], sem)

    def start(r, carry):
        row_copy(r, 0).start()
        row_copy(r, 1).start()
        return carry

    lax.fori_loop(0, tm, start, 0)
    for _ in range(2):
        pltpu.make_async_copy(x_ref.at[pl.ds(0, tm * ROW_CHUNKS)], out_ref.at[pl.ds(0, tm * ROW_CHUNKS)], sem).wait()


def _scatter_rows(x_rows, dest, n_rows):
    n = x_rows.shape[0] // ROW_PITCH
    tm = _tile(n, SCATTER_TILE)
    init = jnp.zeros((n_rows * ROW_PITCH, LANES), x_rows.dtype)
    return pl.pallas_call(
        functools.partial(_scatter_kernel, tm=tm, n=n),
        grid_spec=pltpu.PrefetchScalarGridSpec(
            num_scalar_prefetch=1,
            grid=(n // tm,),
            in_specs=[pl.BlockSpec((tm * ROW_PITCH, LANES), lambda i, d: (i, 0)),
                      pl.BlockSpec(memory_space=pl.ANY)],
            out_specs=pl.BlockSpec(memory_space=pl.ANY),
            scratch_shapes=[pltpu.SemaphoreType.DMA],
        ),
        out_shape=jax.ShapeDtypeStruct((n_rows * ROW_PITCH, LANES), x_rows.dtype),
        input_output_aliases={2: 0},
        compiler_params=_cparams(("arbitrary",)),
    )(dest, x_rows, init)


def _moe_kernel(be_ref, nb_ref, x_ref, wgt_ref, wut_ref, wd_ref, y_ref):
    i = pl.program_id(0)

    @pl.when(i < nb_ref[0])
    def _():
        x = _from_rows(x_ref, MOE_BLOCK).astype(BF16)
        kc = D_MODEL // 4
        g = None
        u = None
        for j in range(4):
            xs = x[:, j * kc:(j + 1) * kc]
            gj = _dot_nt(xs, wgt_ref[:, j * kc:(j + 1) * kc].astype(BF16))
            uj = _dot_nt(xs, wut_ref[:, j * kc:(j + 1) * kc].astype(BF16))
            g = gj if g is None else g + gj
            u = uj if u is None else u + uj
        hid = ((g * _sigmoid(g)) * u).astype(BF16)
        _to_rows(y_ref, _dot(hid, wd_ref[...].astype(BF16)))

    @pl.when(i >= nb_ref[0])
    def _():
        y_ref[...] = jnp.zeros_like(y_ref)


def _moe_ffn(xs, block_e, n_used, wgt_all, wut_all, wd_all, layer):
    n_rows = xs.shape[0] // ROW_PITCH
    bm = MOE_BLOCK
    rows = pl.BlockSpec((bm * ROW_PITCH, LANES), lambda i, be, nb: (i, 0))
    wspec = pl.BlockSpec((None, None, D_EXPERT, D_MODEL), lambda i, be, nb: (layer, be[i], 0, 0))
    return pl.pallas_call(
        _moe_kernel,
        grid_spec=pltpu.PrefetchScalarGridSpec(
            num_scalar_prefetch=2,
            grid=(n_rows // bm,),
            in_specs=[rows, wspec, wspec, wspec],
            out_specs=rows,
        ),
        out_shape=jax.ShapeDtypeStruct((n_rows * ROW_PITCH, LANES), F32),
        compiler_params=_cparams(("arbitrary",)),
    )(block_e, n_used, xs, wgt_all, wut_all, wd_all)


def _combine_kernel(dest_ref, x_ref, w_ref, yb_ref, g_ref, b_ref, o_ref, ob_ref, buf_ref, sem, *, tm, n, alpha):
    base = pl.program_id(0) * tm

    def row_copy(r, k):
        return pltpu.make_async_copy(yb_ref.at[pl.ds(dest_ref[k * n + base + r] * ROW_PITCH, ROW_CHUNKS)],
                                     buf_ref.at[k, pl.ds(r * ROW_PITCH, ROW_CHUNKS)], sem)

    def start(r, carry):
        row_copy(r, 0).start()
        row_copy(r, 1).start()
        return carry

    lax.fori_loop(0, tm, start, 0)
    for k in range(2):
        pltpu.make_async_copy(yb_ref.at[pl.ds(0, tm * ROW_CHUNKS)], buf_ref.at[k, pl.ds(0, tm * ROW_CHUNKS)],
                              sem).wait()
    w = w_ref[...]
    y = w[:, 0:1] * _from_rows(buf_ref.at[0], tm) + w[:, 1:2] * _from_rows(buf_ref.at[1], tm)
    x2 = _layer_norm(alpha * x_ref[...] + y, g_ref[...], b_ref[...])
    o_ref[...] = x2
    ob_ref[...] = x2.astype(BF16)


def _combine_ln(x1, w_col, yb, dest, g_all, b_all, layer, alpha):
    n = x1.shape[0]
    tm = _tile(n, COMBINE_TILE)
    row = pl.BlockSpec((tm, D_MODEL), lambda i, d: (i, 0))
    vec = pl.BlockSpec((None, 1, D_MODEL), lambda i, d: (layer, 0, 0))
    return pl.pallas_call(
        functools.partial(_combine_kernel, tm=tm, n=n, alpha=alpha),
        grid_spec=pltpu.PrefetchScalarGridSpec(
            num_scalar_prefetch=1,
            grid=(n // tm,),
            in_specs=[row, pl.BlockSpec((tm, 2), lambda i, d: (i, 0)), pl.BlockSpec(memory_space=pl.ANY), vec, vec],
            out_specs=[row, row],
            scratch_shapes=[pltpu.VMEM((2, tm * ROW_PITCH, LANES), F32), pltpu.SemaphoreType.DMA],
        ),
        out_shape=[jax.ShapeDtypeStruct((n, D_MODEL), F32), jax.ShapeDtypeStruct((n, D_MODEL), BF16)],
        compiler_params=_cparams(("arbitrary",)),
    )(dest, x1, w_col, yb, g_all, b_all)


def _relayout_w_in(w_in):
    def seg(name):
        a, b = _ORIG[name]
        return w_in[..., a:b]

    def partner(name):
        a0 = _ORIG[name][0]
        half = RET_DK // 2
        parts = []
        for h in range(N_HEADS):
            b = a0 + RET_DK * h
            parts += [-w_in[..., b + half:b + RET_DK], w_in[..., b:b + half]]
        return parts

    lead = w_in.shape[:-1]
    cols = ([seg(nm) for nm in _SEG512] + [seg("gq"), seg("gk"), seg("rq"), seg("rk")]
            + partner("rq") + partner("rk")
            + [seg("glr"), jnp.zeros(lead + (LANES - GLA_RANK + _H_PAD,), w_in.dtype)])
    return jnp.concatenate(cols, axis=-1).astype(BF16)


def kernel(x, positions, w_in, w_gla_gate2, b_gla_gate2, gla_norm, hgrn_lb, hgrn_norm, conv_w, conv_b,
           w_lru_a, b_lru_a, w_lru_x, b_lru_x, lru_lambda, w_branch_up, w_merge_gate, b_merge_gate, w_out,
           ln1_g, ln1_b, w_router, router_bias, w_exp_gate, w_exp_up, w_exp_down, ln2_g, ln2_b):
    bsz, seq, d = x.shape
    depth = w_in.shape[0]
    n = bsz * seq
    alpha = (2 * depth) ** 0.25

    w_in_b = _relayout_w_in(w_in)
    w2_b = jnp.pad(w_gla_gate2, ((0, 0), (0, LANES - GLA_RANK), (0, 0))).astype(BF16)
    b2 = b_gla_gate2.reshape(depth, 1, -1)
    gla_nw = jnp.tile(gla_norm, (1, N_HEADS)).reshape(depth, 1, -1)
    hgrn_nw = jnp.tile(hgrn_norm, (1, N_HEADS)).reshape(depth, 1, -1)
    lb_cum = jnp.cumsum(jax.nn.softmax(hgrn_lb.astype(F32), axis=0), axis=0)
    lower = lb_cum - lb_cum[:1]
    log_lb = jnp.log(lower).reshape(depth, 1, -1)
    log_1m_lb = jnp.log1p(-lower).reshape(depth, 1, -1)
    one_m_lb = (1.0 - lower).reshape(depth, 1, -1)
    wa_b = w_lru_a.astype(BF16)
    wx_b = w_lru_x.astype(BF16)
    wg_b = w_merge_gate.astype(BF16)
    wu_b = w_branch_up.astype(BF16)
    bmg = b_merge_gate.reshape(depth, 4, 1, d)
    w_out_b = w_out.astype(BF16)
    wr_t = w_router.T.astype(BF16)
    bias_col = router_bias.reshape(N_EXPERTS, 1).astype(F32)
    ln1g = ln1_g.reshape(depth, 1, d)
    ln1b = ln1_b.reshape(depth, 1, d)
    ln2g = ln2_g.reshape(depth, 1, d)
    ln2b = ln2_b.reshape(depth, 1, d)
    wgt = jnp.swapaxes(w_exp_gate, 2, 3)
    wut = jnp.swapaxes(w_exp_up, 2, 3)

    half = RET_DK // 2
    inv_freq = ROPE_BASE ** (-jnp.arange(half, dtype=F32) / half)
    invf = jnp.tile(inv_freq, 2 * N_HEADS).reshape(1, N_HEADS * RET_DK)
    cos_t, sin_t = _rope_tables(positions.reshape(n, 1), invf)

    cum, lv = _gated_consts(_tile(seq, GATED_BLOCK))
    ret_c = _tile(seq, RET_BLOCK)
    ret_consts = _ret_consts(ret_c)

    x2d = x.reshape(n, d)
    xb = x2d.astype(BF16)
    for l in range(depth):
        hcat = _in_proj(xb, w_in_b, l)
        o_gla = _gla_branch(hcat, seq, w2_b[l], b2[l], gla_nw[l], cum, lv)
        o_hgrn = _hgrn_branch(hcat, seq, log_lb[l], log_1m_lb[l], one_m_lb[l], hgrn_nw[l], cum, lv)
        o_ret = _ret_branch(hcat, seq, cos_t, sin_t, ret_consts, ret_c)
        o_lru = _lru_branch(hcat, seq, conv_w[l], conv_b[l].reshape(1, -1), wa_b[l], b_lru_a[l].reshape(1, -1),
                            wx_b[l], b_lru_x[l].reshape(1, -1), lru_lambda[l].reshape(1, -1))
        merged = _merge(xb, (o_gla, o_hgrn, o_ret, o_lru), wg_b, wu_b, bmg, l)
        x1, x1_rows, logits_t = _out_proj_ln(merged, w_out_b, x2d, ln1g, ln1b, wr_t, l, alpha)
        idx_t, wt_t = _route(logits_t, bias_col)
        dest, block_e, n_used, n_rows = _dispatch_plan(idx_t, n)
        xs = _scatter_rows(x1_rows, dest, n_rows)
        yb = _moe_ffn(xs, block_e, n_used, wgt, wut, w_exp_down, l)
        x2d, xb = _combine_ln(x1, wt_t.T, yb, dest, ln2g, ln2b, l, alpha)
    return x2d.reshape(bsz, seq, d)
```

```python
import functools
import math

import numpy as np
import jax
import jax.numpy as jnp
from jax import lax
from jax.experimental import pallas as pl
from jax.experimental.pallas import tpu as pltpu

F32 = jnp.float32
BF16 = jnp.bfloat16

D_MODEL = 2048
N_HEADS = 4
DV = 128
BRANCH_WIDTH = 512
GLA_DK = 64
GLA_RANK = 16
GLA_GATE_NORM = 16.0
HGRN_DK = 128
RET_DK = 64
ROPE_BASE = 10000.0
CONV_WIDTH = 4
LRU_C = 8.0
N_EXPERTS = 32
N_GROUPS = 4
EXPERTS_PER_GROUP = 8
D_EXPERT = 704
NORM_EPS = 1e-5

LANES = 128
SUBLANES = 8
ROW_CHUNKS = D_MODEL // LANES
ROW_PITCH = ROW_CHUNKS + 1
VMEM_LIMIT = 56 * 1024 * 1024

_ORIG_SIZES = (("gq", 256), ("gk", 256), ("gv", 512), ("gg", 512), ("glr", 16),
               ("hq", 512), ("hf", 512), ("hi", 512), ("hg", 512),
               ("rq", 256), ("rk", 256), ("rv", 512), ("rg", 512), ("lx", 512), ("ly", 512))
_ORIG = {}
_o = 0
for _n, _w in _ORIG_SIZES:
    _ORIG[_n] = (_o, _o + _w)
    _o += _w

_SEG512 = ("gv", "gg", "hq", "hf", "hi", "hg", "rv", "rg", "lx", "ly")
_SEG256 = ("gq", "gk", "rq", "rk", "rqp", "rkp")
_OFF = {}
_o = 0
for _n in _SEG512:
    _OFF[_n] = _o
    _o += 512
for _n in _SEG256:
    _OFF[_n] = _o
    _o += 256
_OFF["glr"] = _o
_o += LANES
H_COLS = 6912
_H_PAD = H_COLS - _o

GATED_BLOCK = 128
RET_BLOCK = 256
LRU_BLOCK = 256
SEQ_GROUP = 4
DMA_UNROLL = 8
MOE_BLOCK = 256
SCATTER_TILE = 512
COMBINE_TILE = 256
RANK_TILE = 1024


def _tile(n, pref):
    return pref if n % pref == 0 else n


def _cparams(sem):
    return pltpu.CompilerParams(dimension_semantics=sem, vmem_limit_bytes=VMEM_LIMIT)


def _dot(a, b):
    return jnp.dot(a, b, preferred_element_type=F32)


def _dot_nt(a, b):
    return lax.dot_general(a, b, (((1,), (1,)), ((), ())), preferred_element_type=F32)


def _dot_tn(a, b):
    return lax.dot_general(a, b, (((0,), (0,)), ((), ())), preferred_element_type=F32)


def _sigmoid(x):
    return 1.0 / (1.0 + jnp.exp(-x))


def _log_sigmoid(x):
    return jnp.minimum(x, 0.0) - jnp.log(1.0 + jnp.exp(-jnp.abs(x)))


def _softplus(x):
    return jnp.maximum(x, 0.0) + jnp.log(1.0 + jnp.exp(-jnp.abs(x)))


def _layer_norm(y, g, b):
    mu = jnp.mean(y, axis=-1, keepdims=True)
    yc = y - mu
    var = jnp.mean(yc * yc, axis=-1, keepdims=True)
    return yc * lax.rsqrt(var + NORM_EPS) * g + b


def _mm_kernel(x_ref, w_ref, o_ref):
    o_ref[...] = _dot(x_ref[...], w_ref[...]).astype(o_ref.dtype)


def _in_proj(xb, w_all, layer):
    n, k = xb.shape
    cols = w_all.shape[-1]
    tm = _tile(n, 512)
    tn = 2304
    return pl.pallas_call(
        _mm_kernel,
        grid=(cols // tn, n // tm),
        in_specs=[pl.BlockSpec((tm, k), lambda j, i: (i, 0)),
                  pl.BlockSpec((None, k, tn), lambda j, i: (layer, 0, j))],
        out_specs=pl.BlockSpec((tm, tn), lambda j, i: (i, j)),
        out_shape=jax.ShapeDtypeStruct((n, cols), F32),
        compiler_params=_cparams(("arbitrary", "arbitrary")),
    )(xb, w_all)


def _gated_consts(c):
    n16 = c // 16
    i = np.arange(c)
    g = i // 16
    hs = []
    h = 1
    while h <= n16:
        hs.append(h)
        h *= 2
    same16 = g[:, None] == g[None, :]
    mats = [same16 & (i[None, :] <= i[:, None]), same16 & (i[None, :] > i[:, None])]
    gq = np.arange(n16)[:, None]
    gt = g[None, :]
    for h in hs[1:]:
        lo = (gq // h) * h
        mats.append((gt >= lo) & (gt < gq))
        mats.append((gt > gq) & (gt <= lo + h - 1))
    cum = np.concatenate(mats, 0).astype(np.float32)
    lv = -np.ones((c, c), np.int32)
    lv[(g[:, None] == g[None, :]) & (i[None, :] <= i[:, None])] = 0
    gi = g[:, None]
    gj = g[None, :]
    for li, h in enumerate(hs[:-1]):
        m = (gi // (2 * h) == gj // (2 * h)) & ((gi // h) % 2 == 1) & ((gj // h) % 2 == 0)
        lv[m] = li + 1
    return jnp.asarray(cum, BF16), jnp.asarray(lv)


def _exact_select_dot(sel, x):
    hi = x.astype(BF16)
    r1 = x - hi.astype(F32)
    mid = r1.astype(BF16)
    lo = (r1 - mid.astype(F32)).astype(BF16)
    return _dot(sel, hi) + _dot(sel, mid) + _dot(sel, lo)


def _expand_groups(t):
    w = t.shape[1]
    return jnp.concatenate([jnp.broadcast_to(t[g:g + 1, :], (16, w)) for g in range(t.shape[0])], axis=0)


def _gated_core(q, k, la, v_ref, g_ref, nw_ref, cum_ref, lv_ref, states, *, c, dk):
    ex = _exact_select_dot(cum_ref[...], la)
    n16 = c // 16
    nl = 1 + (ex.shape[0] - 2 * c) // (2 * n16)
    c_t = [ex[0:c]]
    c_e = [ex[c:2 * c]]
    for l in range(1, nl):
        r0 = 2 * c + 2 * (l - 1) * n16
        c_t.append(c_t[0] + _expand_groups(ex[r0:r0 + n16]))
        c_e.append(c_e[0] + _expand_groups(ex[r0 + n16:r0 + 2 * n16]))
    qd = [(q * jnp.exp(c_t[l])).astype(BF16) for l in range(nl)]
    k_diag = (k * jnp.exp(-c_t[0])).astype(BF16)
    kd = [(k * jnp.exp(c_e[l])).astype(BF16) for l in range(nl)]
    dec = jnp.exp(c_t[nl - 1][c - 1:c, :])
    lv = lv_ref[...]
    outs = []
    new_states = []
    for h in range(N_HEADS):
        sk = slice(h * dk, (h + 1) * dk)
        sv = slice(h * DV, (h + 1) * DV)
        p = jnp.where(lv == 0, _dot_nt(qd[0][:, sk], k_diag[:, sk]), 0.0)
        for l in range(nl - 1):
            p = p + jnp.where(lv == l + 1, _dot_nt(qd[l][:, sk], kd[l][:, sk]), 0.0)
        vh = v_ref[:, sv].astype(BF16)
        st = states[h]
        o = _dot(p.astype(BF16), vh) + _dot_nt(qd[nl - 1][:, sk], st.astype(BF16))
        new_states.append(st * dec[:, sk] + _dot_tn(vh, kd[nl - 1][:, sk]))
        ms = jnp.mean(o * o, axis=-1, keepdims=True)
        gh = g_ref[:, sv]
        outs.append((o * lax.rsqrt(ms + NORM_EPS) * nw_ref[:, sv] * (gh * _sigmoid(gh))).astype(BF16))
    return outs, new_states


def _load_states(st_ref):
    return [[st_ref[bi, h] for h in range(N_HEADS)] for bi in range(st_ref.shape[0])]


def _store_results(results, o_ref, st_ref):
    for bi, (outs, new_states) in enumerate(results):
        for h in range(N_HEADS):
            o_ref[bi, :, h * DV:(h + 1) * DV] = outs[h]
            st_ref[bi, h] = new_states[h]


def _gla_kernel(q_ref, k_ref, v_ref, g_ref, lr_ref, w2_ref, b2_ref, nw_ref, cum_ref, lv_ref,
                o_ref, st_ref, *, c):
    @pl.when(pl.program_id(1) == 0)
    def _():
        st_ref[...] = jnp.zeros_like(st_ref)

    states = _load_states(st_ref)
    results = []
    for bi in range(q_ref.shape[0]):
        q = q_ref[bi] * (GLA_DK ** -0.5)
        logits = _dot(lr_ref[bi].astype(BF16), w2_ref[...]) + b2_ref[...]
        la = _log_sigmoid(logits) * (1.0 / GLA_GATE_NORM)
        results.append(_gated_core(q, k_ref[bi], la, v_ref.at[bi], g_ref.at[bi], nw_ref, cum_ref, lv_ref,
                                   states[bi], c=c, dk=GLA_DK))
    _store_results(results, o_ref, st_ref)


def _hgrn_kernel(q_ref, f_ref, v_ref, g_ref, llb_ref, l1m_ref, oml_ref, nw_ref, cum_ref, lv_ref,
                 o_ref, st_ref, *, c):
    @pl.when(pl.program_id(1) == 0)
    def _():
        st_ref[...] = jnp.zeros_like(st_ref)

    states = _load_states(st_ref)
    results = []
    for bi in range(q_ref.shape[0]):
        z = f_ref[bi]
        a = llb_ref[...]
        b = l1m_ref[...] + _log_sigmoid(z)
        la = jnp.maximum(a, b) + jnp.log(1.0 + jnp.exp(-jnp.abs(a - b)))
        k = oml_ref[...] * _sigmoid(-z)
        qq = q_ref[bi]
        q = qq * _sigmoid(qq)
        results.append(_gated_core(q, k, la, v_ref.at[bi], g_ref.at[bi], nw_ref, cum_ref, lv_ref,
                                   states[bi], c=c, dk=HGRN_DK))
    _store_results(results, o_ref, st_ref)


def _seq_group(bsz):
    for g in (SEQ_GROUP, 2):
        if bsz % g == 0:
            return g
    return 1


def _col_spec(g, c, width, name):
    cb = _OFF[name] // width
    return pl.BlockSpec((g, c, width), lambda b, j: (b, j, cb))


def _seq_out(g, c, bsz, seq):
    return (pl.BlockSpec((g, c, BRANCH_WIDTH), lambda b, j: (b, j, 0)),
            jax.ShapeDtypeStruct((bsz, seq, BRANCH_WIDTH), BF16))


def _row_spec(width):
    return pl.BlockSpec((1, width), lambda b, j: (0, 0))


def _const_spec(shape):
    nd = len(shape)
    return pl.BlockSpec(shape, lambda b, j: (0,) * nd)


def _gla_branch(hcat, w2, b2, nw, cum, lv):
    bsz, seq, _ = hcat.shape
    c = _tile(seq, GATED_BLOCK)
    g = _seq_group(bsz)
    out_spec, out_shape = _seq_out(g, c, bsz, seq)
    return pl.pallas_call(
        functools.partial(_gla_kernel, c=c),
        grid=(bsz // g, seq // c),
        in_specs=[_col_spec(g, c, 256, "gq"), _col_spec(g, c, 256, "gk"),
                  _col_spec(g, c, 512, "gv"), _col_spec(g, c, 512, "gg"),
                  _col_spec(g, c, LANES, "glr"),
                  _const_spec(w2.shape), _row_spec(256), _row_spec(512),
                  _const_spec(cum.shape), _const_spec(lv.shape)],
        out_specs=out_spec,
        out_shape=out_shape,
        scratch_shapes=[pltpu.VMEM((g, N_HEADS, DV, GLA_DK), F32)],
        compiler_params=_cparams(("arbitrary", "arbitrary")),
    )(hcat, hcat, hcat, hcat, hcat, w2, b2, nw, cum, lv)


def _hgrn_branch(hcat, llb, l1m, oml, nw, cum, lv):
    bsz, seq, _ = hcat.shape
    c = _tile(seq, GATED_BLOCK)
    g = _seq_group(bsz)
    out_spec, out_shape = _seq_out(g, c, bsz, seq)
    return pl.pallas_call(
        functools.partial(_hgrn_kernel, c=c),
        grid=(bsz // g, seq // c),
        in_specs=[_col_spec(g, c, 512, "hq"), _col_spec(g, c, 512, "hf"),
                  _col_spec(g, c, 512, "hi"), _col_spec(g, c, 512, "hg"),
                  _row_spec(512), _row_spec(512), _row_spec(512), _row_spec(512),
                  _const_spec(cum.shape), _const_spec(lv.shape)],
        out_specs=out_spec,
        out_shape=out_shape,
        scratch_shapes=[pltpu.VMEM((g, N_HEADS, DV, HGRN_DK), F32)],
        compiler_params=_cparams(("arbitrary", "arbitrary")),
    )(hcat, hcat, hcat, hcat, llb, l1m, oml, nw, cum, lv)


def _rope_kernel(pos_ref, invf_ref, cos_ref, sin_ref):
    ang = pos_ref[...].astype(F32) * invf_ref[...]
    cos_ref[...] = jnp.cos(ang)
    sin_ref[...] = jnp.sin(ang)


def _rope_tables(pos_col, invf):
    n = pos_col.shape[0]
    tm = _tile(n, 2048)
    w = invf.shape[1]
    return pl.pallas_call(
        _rope_kernel,
        grid=(n // tm,),
        in_specs=[pl.BlockSpec((tm, 1), lambda i: (i, 0)), pl.BlockSpec((1, w), lambda i: (0, 0))],
        out_specs=[pl.BlockSpec((tm, w), lambda i: (i, 0))] * 2,
        out_shape=[jax.ShapeDtypeStruct((n, w), F32)] * 2,
        compiler_params=_cparams(("arbitrary",)),
    )(pos_col, invf)


def _ret_consts(c):
    hh = np.arange(N_HEADS, dtype=np.float64)
    log_gamma = np.log(1.0 - 2.0 ** (-5.0 - hh))
    pos = np.arange(c, dtype=np.float64)
    diff = pos[:, None] - pos[None, :]
    dm = np.where(diff >= 0, np.exp(np.where(diff >= 0, diff, 0.0)[None] * log_gamma[:, None, None]), 0.0)
    col_head = np.arange(N_HEADS * RET_DK) // RET_DK
    qs = np.exp((pos + 1.0)[:, None] * log_gamma[col_head][None, :])
    ks = np.exp((c - 1.0 - pos)[:, None] * log_gamma[col_head][None, :])
    gc = np.exp(c * log_gamma[col_head])[None, :]
    return (jnp.asarray(dm, F32), jnp.asarray(qs, F32), jnp.asarray(ks, F32), jnp.asarray(gc, F32))


def _ret_kernel(q_ref, k_ref, qp_ref, kp_ref, v_ref, g_ref, cos_ref, sin_ref, dm_ref, qs_ref, ks_ref,
                gc_ref, o_ref, st_ref):
    @pl.when(pl.program_id(1) == 0)
    def _():
        st_ref[...] = jnp.zeros_like(st_ref)

    gc = gc_ref[...]
    states = _load_states(st_ref)
    results = []
    for bi in range(q_ref.shape[0]):
        outs = []
        new_states = []
        cs = cos_ref[bi]
        sn = sin_ref[bi]
        q = q_ref[bi] * cs + qp_ref[bi] * sn
        k = (k_ref[bi] * cs + kp_ref[bi] * sn) * (RET_DK ** -0.5)
        qb = q.astype(BF16)
        kb = k.astype(BF16)
        q_dec = (q * qs_ref[...]).astype(BF16)
        k_dec = (k * ks_ref[...]).astype(BF16)
        for h in range(N_HEADS):
            sk = slice(h * RET_DK, (h + 1) * RET_DK)
            sv = slice(h * DV, (h + 1) * DV)
            s = _dot_nt(qb[:, sk], kb[:, sk]) * dm_ref[h]
            vh = v_ref[bi, :, sv].astype(BF16)
            st = states[bi][h]
            o = _dot(s.astype(BF16), vh) + _dot_nt(q_dec[:, sk], st.astype(BF16))
            new_states.append(st * gc[:, sk] + _dot_tn(vh, k_dec[:, sk]))
            ms = jnp.mean(o * o, axis=-1, keepdims=True)
            gh = g_ref[bi, :, sv]
            outs.append((o * lax.rsqrt(ms + NORM_EPS) * (gh * _sigmoid(gh))).astype(BF16))
        results.append((outs, new_states))
    _store_results(results, o_ref, st_ref)


def _ret_branch(hcat, cos_t, sin_t, consts, c):
    bsz, seq, _ = hcat.shape
    g = _seq_group(bsz)
    out_spec, out_shape = _seq_out(g, c, bsz, seq)
    dm, qs, ks, gc = consts
    tab = pl.BlockSpec((g, c, 256), lambda b, j: (b, j, 0))
    return pl.pallas_call(
        _ret_kernel,
        grid=(bsz // g, seq // c),
        in_specs=[_col_spec(g, c, 256, "rq"), _col_spec(g, c, 256, "rk"),
                  _col_spec(g, c, 256, "rqp"), _col_spec(g, c, 256, "rkp"),
                  _col_spec(g, c, 512, "rv"), _col_spec(g, c, 512, "rg"), tab, tab,
                  _const_spec(dm.shape), _const_spec(qs.shape), _const_spec(ks.shape), _row_spec(256)],
        out_specs=out_spec,
        out_shape=out_shape,
        scratch_shapes=[pltpu.VMEM((g, N_HEADS, DV, RET_DK), F32)],
        compiler_params=_cparams(("arbitrary", "arbitrary")),
    )(hcat, hcat, hcat, hcat, hcat, hcat, cos_t, sin_t, dm, qs, ks, gc)


def _lru_kernel(x_ref, y_ref, cw_ref, cb_ref, wa_ref, ba_ref, wx_ref, bx_ref, lam_ref, o_ref,
                halo_ref, h_ref, *, c):
    @pl.when(pl.program_id(1) == 0)
    def _():
        halo_ref[...] = jnp.zeros_like(halo_ref)
        h_ref[...] = jnp.zeros_like(h_ref)

    cw = cw_ref[...]
    sp = _softplus(-lam_ref[...])
    rows = lax.broadcasted_iota(jnp.int32, (c, 1), 0)
    blk = BRANCH_WIDTH // 4
    for bi in range(x_ref.shape[0]):
        x = x_ref[bi]
        x_ext = jnp.concatenate([halo_ref[bi], x], axis=0)
        xc = cw[CONV_WIDTH - 1:CONV_WIDTH, :] * x
        for w in range(CONV_WIDTH - 1):
            shift = CONV_WIDTH - 1 - w
            xc = xc + cw[w:w + 1, :] * pltpu.roll(x_ext, shift, 0)[SUBLANES:, :]
        xc = xc + cb_ref[...]
        halo_ref[bi] = x[c - SUBLANES:, :]

        ra = []
        rx = []
        for g in range(4):
            xg = xc[:, g * blk:(g + 1) * blk].astype(BF16)
            ra.append(_dot(xg, wa_ref[g]))
            rx.append(_dot(xg, wx_ref[g]))
        r = _sigmoid(jnp.concatenate(ra, axis=1) + ba_ref[...])
        ig = _sigmoid(jnp.concatenate(rx, axis=1) + bx_ref[...])
        log_a = -LRU_C * r * sp
        a = jnp.exp(log_a)
        u = jnp.sqrt(-jnp.tanh(log_a) * (a * a + 1.0)) * (ig * xc)

        s = 1
        while s < c:
            keep = rows >= s
            a_sh = jnp.where(keep, pltpu.roll(a, s, 0), 1.0)
            u_sh = jnp.where(keep, pltpu.roll(u, s, 0), 0.0)
            u = a * u_sh + u
            a = a * a_sh
            s *= 2
        hs = u + a * h_ref[bi, 0:1, :]
        h_ref[bi] = jnp.broadcast_to(hs[c - 1:c, :], (SUBLANES, BRANCH_WIDTH))
        y = y_ref[bi]
        gelu = 0.5 * y * (1.0 + jnp.tanh(math.sqrt(2.0 / math.pi) * (y + 0.044715 * (y * y * y))))
        o_ref[bi] = (hs * gelu).astype(o_ref.dtype)


def _lru_branch(hcat, cw, cb, wa, ba, wx, bx, lam):
    bsz, seq, _ = hcat.shape
    c = _tile(seq, LRU_BLOCK)
    g = _seq_group(bsz)
    out_spec, out_shape = _seq_out(g, c, bsz, seq)
    return pl.pallas_call(
        functools.partial(_lru_kernel, c=c),
        grid=(bsz // g, seq // c),
        in_specs=[_col_spec(g, c, 512, "lx"), _col_spec(g, c, 512, "ly"),
                  _const_spec(cw.shape), _row_spec(512), _const_spec(wa.shape), _row_spec(512),
                  _const_spec(wx.shape), _row_spec(512), _row_spec(512)],
        out_specs=out_spec,
        out_shape=out_shape,
        scratch_shapes=[pltpu.VMEM((g, SUBLANES, BRANCH_WIDTH), F32), pltpu.VMEM((g, SUBLANES, BRANCH_WIDTH), F32)],
        compiler_params=_cparams(("arbitrary", "arbitrary")),
    )(hcat, hcat, cw, cb, wa, ba, wx, bx, lam)


def _merge_kernel(x_ref, o0_ref, o1_ref, o2_ref, o3_ref, wg_ref, wu_ref, b_ref, out_ref):
    x = x_ref[...]
    acc = None
    for br, o_ref in enumerate((o0_ref, o1_ref, o2_ref, o3_ref)):
        gate = _sigmoid(_dot(x, wg_ref[br]) + b_ref[br])
        term = gate * _dot(o_ref[...], wu_ref[br])
        acc = term if acc is None else acc + term
    out_ref[...] = acc.astype(out_ref.dtype)


def _merge(xb, outs, wg_all, wu_all, b_all, layer):
    n = xb.shape[0]
    tm = _tile(n, 1024)
    tn = 512
    ospec = pl.BlockSpec((tm, BRANCH_WIDTH), lambda i, j: (i, 0))
    return pl.pallas_call(
        _merge_kernel,
        grid=(n // tm, D_MODEL // tn),
        in_specs=[pl.BlockSpec((tm, D_MODEL), lambda i, j: (i, 0)), ospec, ospec, ospec, ospec,
                  pl.BlockSpec((None, 4, D_MODEL, tn), lambda i, j: (layer, 0, 0, j)),
                  pl.BlockSpec((None, 4, BRANCH_WIDTH, tn), lambda i, j: (layer, 0, 0, j)),
                  pl.BlockSpec((None, 4, 1, tn), lambda i, j: (layer, 0, 0, j))],
        out_specs=pl.BlockSpec((tm, tn), lambda i, j: (i, j)),
        out_shape=jax.ShapeDtypeStruct((n, D_MODEL), BF16),
        compiler_params=_cparams(("arbitrary", "arbitrary")),
    )(xb, *outs, wg_all, wu_all, b_all)


def _to_rows(ref, val):
    m = val.shape[0]
    for j in range(ROW_CHUNKS):
        ref[pl.ds(j, m, stride=ROW_PITCH), :] = val[:, j * LANES:(j + 1) * LANES]
    for j in range(ROW_CHUNKS, ROW_PITCH):
        ref[pl.ds(j, m, stride=ROW_PITCH), :] = jnp.zeros((m, LANES), val.dtype)


def _from_rows(ref, m):
    return jnp.concatenate([ref[pl.ds(j, m, stride=ROW_PITCH), :] for j in range(ROW_CHUNKS)], axis=1)


def _outln_kernel(m_ref, w_ref, x_ref, g_ref, b_ref, wr_ref, x1_ref, x1r_ref, lg_ref, *, alpha):
    y = alpha * x_ref[...] + _dot(m_ref[...], w_ref[...])
    x1 = _layer_norm(y, g_ref[...], b_ref[...])
    x1_ref[...] = x1
    _to_rows(x1r_ref, x1)
    lg_ref[...] = _dot_nt(wr_ref[...], x1.astype(BF16))


def _out_proj_ln(merged, w_all, x, g_all, b_all, wr_t, layer, alpha):
    n = x.shape[0]
    tm = _tile(n, 512)
    row = pl.BlockSpec((tm, D_MODEL), lambda i: (i, 0))
    vec = pl.BlockSpec((None, 1, D_MODEL), lambda i: (layer, 0, 0))
    return pl.pallas_call(
        functools.partial(_outln_kernel, alpha=alpha),
        grid=(n // tm,),
        in_specs=[row, pl.BlockSpec((None, D_MODEL, D_MODEL), lambda i: (layer, 0, 0)), row, vec, vec,
                  pl.BlockSpec((N_EXPERTS, D_MODEL), lambda i: (0, 0))],
        out_specs=[row, pl.BlockSpec((tm * ROW_PITCH, LANES), lambda i: (i, 0)),
                   pl.BlockSpec((N_EXPERTS, tm), lambda i: (0, i))],
        out_shape=[jax.ShapeDtypeStruct((n, D_MODEL), F32), jax.ShapeDtypeStruct((n * ROW_PITCH, LANES), F32),
                   jax.ShapeDtypeStruct((N_EXPERTS, n), F32)],
        compiler_params=_cparams(("arbitrary",)),
    )(merged, w_all, x, g_all, b_all, wr_t)


def _route_kernel(lg_ref, bias_ref, idx_ref, wt_ref):
    s = _sigmoid(lg_ref[...])
    sel = s + bias_ref[...]
    tn = s.shape[1]
    rows = lax.broadcasted_iota(jnp.int32, (EXPERTS_PER_GROUP, tn), 0)
    neg = jnp.float32(-jnp.inf)

    def first_max(blk):
        m = jnp.max(blk, axis=0, keepdims=True)
        i = jnp.min(jnp.where(blk == m, rows, EXPERTS_PER_GROUP), axis=0, keepdims=True)
        return m, i

    best = None
    for g in range(N_GROUPS):
        sl = slice(g * EXPERTS_PER_GROUP, (g + 1) * EXPERTS_PER_GROUP)
        blk = sel[sl]
        sg = s[sl]
        m1, i1 = first_max(blk)
        m2, i2 = first_max(jnp.where(rows == i1, neg, blk))
        w1 = jnp.sum(jnp.where(rows == i1, sg, 0.0), axis=0, keepdims=True)
        w2 = jnp.sum(jnp.where(rows == i2, sg, 0.0), axis=0, keepdims=True)
        cand = (m1 + m2, i1 + g * EXPERTS_PER_GROUP, i2 + g * EXPERTS_PER_GROUP, w1, w2)
        if best is None:
            best = cand
        else:
            take = cand[0] > best[0]
            best = tuple(jnp.where(take, cn, bs) for cn, bs in zip(cand, best))
    _, e1, e2, w1, w2 = best
    tot = w1 + w2
    idx_ref[0:1, :] = e1
    idx_ref[1:2, :] = e2
    wt_ref[0:1, :] = w1 / tot
    wt_ref[1:2, :] = w2 / tot


def _route(logits_t, bias_col):
    n = logits_t.shape[1]
    tn = _tile(n, 2048)
    return pl.pallas_call(
        _route_kernel,
        grid=(n // tn,),
        in_specs=[pl.BlockSpec((N_EXPERTS, tn), lambda i: (0, i)),
                  pl.BlockSpec((N_EXPERTS, 1), lambda i: (0, 0))],
        out_specs=[pl.BlockSpec((2, tn), lambda i: (0, i))] * 2,
        out_shape=[jax.ShapeDtypeStruct((2, n), jnp.int32), jax.ShapeDtypeStruct((2, n), F32)],
        compiler_params=_cparams(("arbitrary",)),
    )(logits_t, bias_col)


def _rank_kernel(e_ref, u_ref, rank_ref, cnt_ref, carry_ref):
    @pl.when(pl.program_id(0) == 0)
    def _():
        carry_ref[...] = jnp.zeros_like(carry_ref)

    e = e_ref[...]
    tn = e.shape[1]
    rows = lax.broadcasted_iota(jnp.int32, (N_EXPERTS, tn), 0)
    onehot = jnp.where(rows == e, 1.0, 0.0)
    before = _dot(onehot.astype(BF16), u_ref[...])
    carry = carry_ref[...]
    rank_ref[...] = jnp.sum(onehot * (before + carry), axis=0, keepdims=True).astype(jnp.int32)
    carry = carry + jnp.sum(onehot, axis=1, keepdims=True)
    carry_ref[...] = carry
    cnt_ref[...] = carry


def _rank(e_row):
    m = e_row.shape[1]
    tn = _tile(m, RANK_TILE)
    upper = jnp.asarray(np.triu(np.ones((tn, tn), np.float32), 1), BF16)
    return pl.pallas_call(
        _rank_kernel,
        grid=(m // tn,),
        in_specs=[pl.BlockSpec((1, tn), lambda i: (0, i)), pl.BlockSpec((tn, tn), lambda i: (0, 0))],
        out_specs=[pl.BlockSpec((1, tn), lambda i: (0, i)), pl.BlockSpec((N_EXPERTS, 1), lambda i: (0, 0))],
        out_shape=[jax.ShapeDtypeStruct((1, m), jnp.int32), jax.ShapeDtypeStruct((N_EXPERTS, 1), F32)],
        scratch_shapes=[pltpu.VMEM((N_EXPERTS, 1), F32)],
        compiler_params=_cparams(("arbitrary",)),
    )(e_row, upper)


def _dispatch_plan(idx_t, n):
    m = 2 * n
    bm = MOE_BLOCK
    e_row = idx_t.reshape(1, m)
    rank, cnt = _rank(e_row)
    counts = cnt.reshape(N_EXPERTS).astype(jnp.int32)
    padded = (counts + bm - 1) // bm * bm
    pad_end = jnp.cumsum(padded)
    pad_start = pad_end - padded
    experts = jnp.arange(N_EXPERTS, dtype=jnp.int32)
    start_of = jnp.sum(jnp.where(e_row == experts[:, None], pad_start[:, None], 0), axis=0)
    dest = (rank.reshape(m) + start_of).astype(jnp.int32)
    n_rows = (m + N_EXPERTS * (bm - 1) + bm - 1) // bm * bm
    n_blocks = n_rows // bm
    starts = jnp.arange(n_blocks, dtype=jnp.int32) * bm
    block_e = jnp.minimum(jnp.sum((pad_end[None, :] <= starts[:, None]).astype(jnp.int32), axis=1),
                          N_EXPERTS - 1).astype(jnp.int32)
    n_used = (pad_end[-1] // bm).astype(jnp.int32).reshape(1)
    return dest, block_e, n_used, n_rows


def _scatter_kernel(dest_ref, x_ref, init_ref, out_ref, sem, *, tm, n):
    del init_ref
    base = pl.program_id(0) * tm

    def row_copy(r, k):
        return pltpu.make_async_copy(x_ref.at[pl.ds(r * ROW_PITCH, ROW_CHUNKS)],
                                     out_ref.at[pl.ds(dest_ref[k * n + base + r] * ROW_PITCH, ROW_CHUNKS)], sem)

    def start(r, carry):
        row_copy(r, 0).start()
        row_copy(r, 1).start()
        return carry

    lax.fori_loop(0, tm, start, 0, unroll=DMA_UNROLL)
    for _ in range(2):
        pltpu.make_async_copy(x_ref.at[pl.ds(0, tm * ROW_CHUNKS)], out_ref.at[pl.ds(0, tm * ROW_CHUNKS)], sem).wait()


def _scatter_rows(x_rows, dest, n_rows):
    n = x_rows.shape[0] // ROW_PITCH
    tm = _tile(n, SCATTER_TILE)
    init = jnp.zeros((n_rows * ROW_PITCH, LANES), x_rows.dtype)
    return pl.pallas_call(
        functools.partial(_scatter_kernel, tm=tm, n=n),
        grid_spec=pltpu.PrefetchScalarGridSpec(
            num_scalar_prefetch=1,
            grid=(n // tm,),
            in_specs=[pl.BlockSpec((tm * ROW_PITCH, LANES), lambda i, d: (i, 0)),
                      pl.BlockSpec(memory_space=pl.ANY)],
            out_specs=pl.BlockSpec(memory_space=pl.ANY),
            scratch_shapes=[pltpu.SemaphoreType.DMA],
        ),
        out_shape=jax.ShapeDtypeStruct((n_rows * ROW_PITCH, LANES), x_rows.dtype),
        input_output_aliases={2: 0},
        compiler_params=_cparams(("arbitrary",)),
    )(dest, x_rows, init)


def _moe_kernel(be_ref, nb_ref, x_ref, wgt_ref, wut_ref, wd_ref, y_ref):
    i = pl.program_id(0)

    @pl.when(i < nb_ref[0])
    def _():
        x = _from_rows(x_ref, MOE_BLOCK).astype(BF16)
        kc = D_MODEL // 4
        g = None
        u = None
        for j in range(4):
            xs = x[:, j * kc:(j + 1) * kc]
            gj = _dot_nt(xs, wgt_ref[:, j * kc:(j + 1) * kc].astype(BF16))
            uj = _dot_nt(xs, wut_ref[:, j * kc:(j + 1) * kc].astype(BF16))
            g = gj if g is None else g + gj
            u = uj if u is None else u + uj
        hid = ((g * _sigmoid(g)) * u).astype(BF16)
        _to_rows(y_ref, _dot(hid, wd_ref[...].astype(BF16)))

    @pl.when(i >= nb_ref[0])
    def _():
        y_ref[...] = jnp.zeros_like(y_ref)


def _moe_ffn(xs, block_e, n_used, wgt_all, wut_all, wd_all, layer):
    n_rows = xs.shape[0] // ROW_PITCH
    bm = MOE_BLOCK
    rows = pl.BlockSpec((bm * ROW_PITCH, LANES), lambda i, be, nb: (i, 0))
    wspec = pl.BlockSpec((None, None, D_EXPERT, D_MODEL), lambda i, be, nb: (layer, be[i], 0, 0))
    return pl.pallas_call(
        _moe_kernel,
        grid_spec=pltpu.PrefetchScalarGridSpec(
            num_scalar_prefetch=2,
            grid=(n_rows // bm,),
            in_specs=[rows, wspec, wspec, wspec],
            out_specs=rows,
        ),
        out_shape=jax.ShapeDtypeStruct((n_rows * ROW_PITCH, LANES), F32),
        compiler_params=_cparams(("arbitrary",)),
    )(block_e, n_used, xs, wgt_all, wut_all, wd_all)


def _combine_kernel(dest_ref, x_ref, w_ref, yb_ref, g_ref, b_ref, o_ref, ob_ref, buf_ref, sem, *, tm, n, alpha):
    base = pl.program_id(0) * tm

    def row_copy(r, k):
        return pltpu.make_async_copy(yb_ref.at[pl.ds(dest_ref[k * n + base + r] * ROW_PITCH, ROW_CHUNKS)],
                                     buf_ref.at[k, pl.ds(r * ROW_PITCH, ROW_CHUNKS)], sem)

    def start(r, carry):
        row_copy(r, 0).start()
        row_copy(r, 1).start()
        return carry

    lax.fori_loop(0, tm, start, 0, unroll=DMA_UNROLL)
    for k in range(2):
        pltpu.make_async_copy(yb_ref.at[pl.ds(0, tm * ROW_CHUNKS)], buf_ref.at[k, pl.ds(0, tm * ROW_CHUNKS)],
                              sem).wait()
    w = w_ref[...]
    y = w[:, 0:1] * _from_rows(buf_ref.at[0], tm) + w[:, 1:2] * _from_rows(buf_ref.at[1], tm)
    x2 = _layer_norm(alpha * x_ref[...] + y, g_ref[...], b_ref[...])
    o_ref[...] = x2
    ob_ref[...] = x2.astype(BF16)


def _combine_ln(x1, w_col, yb, dest, g_all, b_all, layer, alpha):
    n = x1.shape[0]
    tm = _tile(n, COMBINE_TILE)
    row = pl.BlockSpec((tm, D_MODEL), lambda i, d: (i, 0))
    vec = pl.BlockSpec((None, 1, D_MODEL), lambda i, d: (layer, 0, 0))
    return pl.pallas_call(
        functools.partial(_combine_kernel, tm=tm, n=n, alpha=alpha),
        grid_spec=pltpu.PrefetchScalarGridSpec(
            num_scalar_prefetch=1,
            grid=(n // tm,),
            in_specs=[row, pl.BlockSpec((tm, 2), lambda i, d: (i, 0)), pl.BlockSpec(memory_space=pl.ANY), vec, vec],
            out_specs=[row, row],
            scratch_shapes=[pltpu.VMEM((2, tm * ROW_PITCH, LANES), F32), pltpu.SemaphoreType.DMA],
        ),
        out_shape=[jax.ShapeDtypeStruct((n, D_MODEL), F32), jax.ShapeDtypeStruct((n, D_MODEL), BF16)],
        compiler_params=_cparams(("arbitrary",)),
    )(dest, x1, w_col, yb, g_all, b_all)


def _relayout_w_in(w_in):
    def seg(name):
        a, b = _ORIG[name]
        return w_in[..., a:b]

    def partner(name):
        a0 = _ORIG[name][0]
        half = RET_DK // 2
        parts = []
        for h in range(N_HEADS):
            b = a0 + RET_DK * h
            parts += [-w_in[..., b + half:b + RET_DK], w_in[..., b:b + half]]
        return parts

    lead = w_in.shape[:-1]
    cols = ([seg(nm) for nm in _SEG512] + [seg("gq"), seg("gk"), seg("rq"), seg("rk")]
            + partner("rq") + partner("rk")
            + [seg("glr"), jnp.zeros(lead + (LANES - GLA_RANK + _H_PAD,), w_in.dtype)])
    return jnp.concatenate(cols, axis=-1).astype(BF16)


def kernel(x, positions, w_in, w_gla_gate2, b_gla_gate2, gla_norm, hgrn_lb, hgrn_norm, conv_w, conv_b,
           w_lru_a, b_lru_a, w_lru_x, b_lru_x, lru_lambda, w_branch_up, w_merge_gate, b_merge_gate, w_out,
           ln1_g, ln1_b, w_router, router_bias, w_exp_gate, w_exp_up, w_exp_down, ln2_g, ln2_b):
    bsz, seq, d = x.shape
    depth = w_in.shape[0]
    n = bsz * seq
    alpha = (2 * depth) ** 0.25

    w_in_b = _relayout_w_in(w_in)
    w2_b = jnp.pad(w_gla_gate2, ((0, 0), (0, LANES - GLA_RANK), (0, 0))).astype(BF16)
    b2 = b_gla_gate2.reshape(depth, 1, -1)
    gla_nw = jnp.tile(gla_norm, (1, N_HEADS)).reshape(depth, 1, -1)
    hgrn_nw = jnp.tile(hgrn_norm, (1, N_HEADS)).reshape(depth, 1, -1)
    lb_cum = jnp.cumsum(jax.nn.softmax(hgrn_lb.astype(F32), axis=0), axis=0)
    lower = lb_cum - lb_cum[:1]
    log_lb = jnp.log(lower).reshape(depth, 1, -1)
    log_1m_lb = jnp.log1p(-lower).reshape(depth, 1, -1)
    one_m_lb = (1.0 - lower).reshape(depth, 1, -1)
    wa_b = w_lru_a.astype(BF16)
    wx_b = w_lru_x.astype(BF16)
    wg_b = w_merge_gate.astype(BF16)
    wu_b = w_branch_up.astype(BF16)
    bmg = b_merge_gate.reshape(depth, 4, 1, d)
    w_out_b = w_out.astype(BF16)
    wr_t = w_router.T.astype(BF16)
    bias_col = router_bias.reshape(N_EXPERTS, 1).astype(F32)
    ln1g = ln1_g.reshape(depth, 1, d)
    ln1b = ln1_b.reshape(depth, 1, d)
    ln2g = ln2_g.reshape(depth, 1, d)
    ln2b = ln2_b.reshape(depth, 1, d)
    wgt = jnp.swapaxes(w_exp_gate, 2, 3)
    wut = jnp.swapaxes(w_exp_up, 2, 3)

    half = RET_DK // 2
    inv_freq = ROPE_BASE ** (-jnp.arange(half, dtype=F32) / half)
    invf = jnp.tile(inv_freq, 2 * N_HEADS).reshape(1, N_HEADS * RET_DK)
    cos_t, sin_t = _rope_tables(positions.reshape(n, 1), invf)
    cos_t = cos_t.reshape(bsz, seq, -1)
    sin_t = sin_t.reshape(bsz, seq, -1)

    cum, lv = _gated_consts(_tile(seq, GATED_BLOCK))
    ret_c = _tile(seq, RET_BLOCK)
    ret_consts = _ret_consts(ret_c)

    x2d = x.reshape(n, d)
    xb = x2d.astype(BF16)
    for l in range(depth):
        hcat = _in_proj(xb, w_in_b, l).reshape(bsz, seq, H_COLS)
        o_gla = _gla_branch(hcat, w2_b[l], b2[l], gla_nw[l], cum, lv)
        o_hgrn = _hgrn_branch(hcat, log_lb[l], log_1m_lb[l], one_m_lb[l], hgrn_nw[l], cum, lv)
        o_ret = _ret_branch(hcat, cos_t, sin_t, ret_consts, ret_c)
        o_lru = _lru_branch(hcat, conv_w[l], conv_b[l].reshape(1, -1), wa_b[l], b_lru_a[l].reshape(1, -1),
                            wx_b[l], b_lru_x[l].reshape(1, -1), lru_lambda[l].reshape(1, -1))
        outs = tuple(o.reshape(n, BRANCH_WIDTH) for o in (o_gla, o_hgrn, o_ret, o_lru))
        merged = _merge(xb, outs, wg_b, wu_b, bmg, l)
        x1, x1_rows, logits_t = _out_proj_ln(merged, w_out_b, x2d, ln1g, ln1b, wr_t, l, alpha)
        idx_t, wt_t = _route(logits_t, bias_col)
        dest, block_e, n_used, n_rows = _dispatch_plan(idx_t, n)
        xs = _scatter_rows(x1_rows, dest, n_rows)
        yb = _moe_ffn(xs, block_e, n_used, wgt, wut, w_exp_down, l)
        x2d, xb = _combine_ln(x1, wt_t.T, yb, dest, ln2g, ln2b, l, alpha)
    return x2d.reshape(bsz, seq, d)
```

```python
import functools
import math

import numpy as np
import jax
import jax.numpy as jnp
from jax import lax
from jax.experimental import pallas as pl
from jax.experimental.pallas import tpu as pltpu

F32 = jnp.float32
BF16 = jnp.bfloat16

D_MODEL = 2048
N_HEADS = 4
DV = 128
BRANCH_WIDTH = 512
GLA_DK = 64
GLA_RANK = 16
GLA_GATE_NORM = 16.0
HGRN_DK = 128
RET_DK = 64
ROPE_BASE = 10000.0
CONV_WIDTH = 4
LRU_C = 8.0
N_EXPERTS = 32
N_GROUPS = 4
EXPERTS_PER_GROUP = 8
D_EXPERT = 704
NORM_EPS = 1e-5

LANES = 128
SUBLANES = 8
ROW_CHUNKS = D_MODEL // (2 * LANES)
ROW_PITCH = ROW_CHUNKS + 1
VMEM_LIMIT = 56 * 1024 * 1024

_ORIG_SIZES = (("gq", 256), ("gk", 256), ("gv", 512), ("gg", 512), ("glr", 16),
               ("hq", 512), ("hf", 512), ("hi", 512), ("hg", 512),
               ("rq", 256), ("rk", 256), ("rv", 512), ("rg", 512), ("lx", 512), ("ly", 512))
_ORIG = {}
_o = 0
for _n, _w in _ORIG_SIZES:
    _ORIG[_n] = (_o, _o + _w)
    _o += _w

_SEG512 = ("gv", "gg", "hq", "hf", "hi", "hg", "rv", "rg", "lx", "ly")
_SEG256 = ("gq", "gk", "rq", "rk", "rqp", "rkp")
_OFF = {}
_o = 0
for _n in _SEG512:
    _OFF[_n] = _o
    _o += 512
for _n in _SEG256:
    _OFF[_n] = _o
    _o += 256
_OFF["glr"] = _o
_o += LANES
H_COLS = 6912
_H_PAD = H_COLS - _o

GATED_BLOCK = 128
RET_BLOCK = 256
LRU_BLOCK = 256
SEQ_GROUP = 4
DMA_UNROLL = 8
MOE_BLOCK = 256
SCATTER_TILE = 512
COMBINE_TILE = 256
RANK_TILE = 1024


def _tile(n, pref):
    return pref if n % pref == 0 else n


def _cparams(sem):
    return pltpu.CompilerParams(dimension_semantics=sem, vmem_limit_bytes=VMEM_LIMIT)


def _dot(a, b):
    return jnp.dot(a, b, preferred_element_type=F32)


def _dot_nt(a, b):
    return lax.dot_general(a, b, (((1,), (1,)), ((), ())), preferred_element_type=F32)


def _dot_tn(a, b):
    return lax.dot_general(a, b, (((0,), (0,)), ((), ())), preferred_element_type=F32)


def _sigmoid(x):
    return 1.0 / (1.0 + jnp.exp(-x))


def _log_sigmoid(x):
    return jnp.minimum(x, 0.0) - jnp.log(1.0 + jnp.exp(-jnp.abs(x)))


def _softplus(x):
    return jnp.maximum(x, 0.0) + jnp.log(1.0 + jnp.exp(-jnp.abs(x)))


def _layer_norm(y, g, b):
    mu = jnp.mean(y, axis=-1, keepdims=True)
    yc = y - mu
    var = jnp.mean(yc * yc, axis=-1, keepdims=True)
    return yc * lax.rsqrt(var + NORM_EPS) * g + b


def _mm_kernel(x_ref, w_ref, o_ref):
    o_ref[...] = _dot(x_ref[...], w_ref[...]).astype(o_ref.dtype)


def _in_proj(xb, w_all, layer):
    n, k = xb.shape
    cols = w_all.shape[-1]
    tm = _tile(n, 512)
    tn = 2304
    return pl.pallas_call(
        _mm_kernel,
        grid=(cols // tn, n // tm),
        in_specs=[pl.BlockSpec((tm, k), lambda j, i: (i, 0)),
                  pl.BlockSpec((None, k, tn), lambda j, i: (layer, 0, j))],
        out_specs=pl.BlockSpec((tm, tn), lambda j, i: (i, j)),
        out_shape=jax.ShapeDtypeStruct((n, cols), F32),
        compiler_params=_cparams(("arbitrary", "arbitrary")),
    )(xb, w_all)


def _gated_consts(c):
    n16 = c // 16
    i = np.arange(c)
    g = i // 16
    hs = []
    h = 1
    while h <= n16:
        hs.append(h)
        h *= 2
    same16 = g[:, None] == g[None, :]
    mats = [same16 & (i[None, :] <= i[:, None]), same16 & (i[None, :] > i[:, None])]
    gq = np.arange(n16)[:, None]
    gt = g[None, :]
    for h in hs[1:]:
        lo = (gq // h) * h
        mats.append((gt >= lo) & (gt < gq))
        mats.append((gt > gq) & (gt <= lo + h - 1))
    cum = np.concatenate(mats, 0).astype(np.float32)
    lv = -np.ones((c, c), np.int32)
    lv[(g[:, None] == g[None, :]) & (i[None, :] <= i[:, None])] = 0
    gi = g[:, None]
    gj = g[None, :]
    for li, h in enumerate(hs[:-1]):
        m = (gi // (2 * h) == gj // (2 * h)) & ((gi // h) % 2 == 1) & ((gj // h) % 2 == 0)
        lv[m] = li + 1
    return jnp.asarray(cum, BF16), jnp.asarray(lv)


def _exact_select_dot(sel, x):
    hi = x.astype(BF16)
    r1 = x - hi.astype(F32)
    mid = r1.astype(BF16)
    lo = (r1 - mid.astype(F32)).astype(BF16)
    return _dot(sel, hi) + _dot(sel, mid) + _dot(sel, lo)


def _expand_groups(t):
    w = t.shape[1]
    return jnp.concatenate([jnp.broadcast_to(t[g:g + 1, :], (16, w)) for g in range(t.shape[0])], axis=0)


def _gated_core(q, k, la, v_ref, g_ref, nw_ref, cum_ref, lv_ref, states, *, c, dk):
    ex = _exact_select_dot(cum_ref[...], la)
    n16 = c // 16
    nl = 1 + (ex.shape[0] - 2 * c) // (2 * n16)
    c_t = [ex[0:c]]
    c_e = [ex[c:2 * c]]
    for l in range(1, nl):
        r0 = 2 * c + 2 * (l - 1) * n16
        c_t.append(c_t[0] + _expand_groups(ex[r0:r0 + n16]))
        c_e.append(c_e[0] + _expand_groups(ex[r0 + n16:r0 + 2 * n16]))
    qd = [(q * jnp.exp(c_t[l])).astype(BF16) for l in range(nl)]
    k_diag = (k * jnp.exp(-c_t[0])).astype(BF16)
    kd = [(k * jnp.exp(c_e[l])).astype(BF16) for l in range(nl)]
    dec = jnp.exp(c_t[nl - 1][c - 1:c, :])
    lv = lv_ref[...]
    outs = []
    new_states = []
    for h in range(N_HEADS):
        sk = slice(h * dk, (h + 1) * dk)
        sv = slice(h * DV, (h + 1) * DV)
        p = jnp.where(lv == 0, _dot_nt(qd[0][:, sk], k_diag[:, sk]), 0.0)
        for l in range(nl - 1):
            p = p + jnp.where(lv == l + 1, _dot_nt(qd[l][:, sk], kd[l][:, sk]), 0.0)
        vh = v_ref[:, sv].astype(BF16)
        st = states[h]
        o = _dot(p.astype(BF16), vh) + _dot_nt(qd[nl - 1][:, sk], st.astype(BF16))
        new_states.append(st * dec[:, sk] + _dot_tn(vh, kd[nl - 1][:, sk]))
        ms = jnp.mean(o * o, axis=-1, keepdims=True)
        gh = g_ref[:, sv]
        outs.append((o * lax.rsqrt(ms + NORM_EPS) * nw_ref[:, sv] * (gh * _sigmoid(gh))).astype(BF16))
    return outs, new_states


def _load_states(st_ref):
    return [[st_ref[bi, h] for h in range(N_HEADS)] for bi in range(st_ref.shape[0])]


def _store_results(results, o_ref, st_ref):
    for bi, (outs, new_states) in enumerate(results):
        for h in range(N_HEADS):
            o_ref[bi, :, h * DV:(h + 1) * DV] = outs[h]
            st_ref[bi, h] = new_states[h]


def _gla_kernel(q_ref, k_ref, v_ref, g_ref, lr_ref, w2_ref, b2_ref, nw_ref, cum_ref, lv_ref,
                o_ref, st_ref, *, c):
    @pl.when(pl.program_id(1) == 0)
    def _():
        st_ref[...] = jnp.zeros_like(st_ref)

    states = _load_states(st_ref)
    results = []
    for bi in range(q_ref.shape[0]):
        q = q_ref[bi] * (GLA_DK ** -0.5)
        logits = _dot(lr_ref[bi].astype(BF16), w2_ref[...]) + b2_ref[...]
        la = _log_sigmoid(logits) * (1.0 / GLA_GATE_NORM)
        results.append(_gated_core(q, k_ref[bi], la, v_ref.at[bi], g_ref.at[bi], nw_ref, cum_ref, lv_ref,
                                   states[bi], c=c, dk=GLA_DK))
    _store_results(results, o_ref, st_ref)


def _hgrn_kernel(q_ref, f_ref, v_ref, g_ref, llb_ref, l1m_ref, oml_ref, nw_ref, cum_ref, lv_ref,
                 o_ref, st_ref, *, c):
    @pl.when(pl.program_id(1) == 0)
    def _():
        st_ref[...] = jnp.zeros_like(st_ref)

    states = _load_states(st_ref)
    results = []
    for bi in range(q_ref.shape[0]):
        z = f_ref[bi]
        a = llb_ref[...]
        b = l1m_ref[...] + _log_sigmoid(z)
        la = jnp.maximum(a, b) + jnp.log(1.0 + jnp.exp(-jnp.abs(a - b)))
        k = oml_ref[...] * _sigmoid(-z)
        qq = q_ref[bi]
        q = qq * _sigmoid(qq)
        results.append(_gated_core(q, k, la, v_ref.at[bi], g_ref.at[bi], nw_ref, cum_ref, lv_ref,
                                   states[bi], c=c, dk=HGRN_DK))
    _store_results(results, o_ref, st_ref)


def _seq_group(bsz):
    for g in (SEQ_GROUP, 2):
        if bsz % g == 0:
            return g
    return 1


def _col_spec(g, c, width, name):
    cb = _OFF[name] // width
    return pl.BlockSpec((g, c, width), lambda b, j: (b, j, cb))


def _seq_out(g, c, bsz, seq):
    return (pl.BlockSpec((g, c, BRANCH_WIDTH), lambda b, j: (b, j, 0)),
            jax.ShapeDtypeStruct((bsz, seq, BRANCH_WIDTH), BF16))


def _row_spec(width):
    return pl.BlockSpec((1, width), lambda b, j: (0, 0))


def _const_spec(shape):
    nd = len(shape)
    return pl.BlockSpec(shape, lambda b, j: (0,) * nd)


def _gla_branch(hcat, w2, b2, nw, cum, lv):
    bsz, seq, _ = hcat.shape
    c = _tile(seq, GATED_BLOCK)
    g = _seq_group(bsz)
    out_spec, out_shape = _seq_out(g, c, bsz, seq)
    return pl.pallas_call(
        functools.partial(_gla_kernel, c=c),
        grid=(bsz // g, seq // c),
        in_specs=[_col_spec(g, c, 256, "gq"), _col_spec(g, c, 256, "gk"),
                  _col_spec(g, c, 512, "gv"), _col_spec(g, c, 512, "gg"),
                  _col_spec(g, c, LANES, "glr"),
                  _const_spec(w2.shape), _row_spec(256), _row_spec(512),
                  _const_spec(cum.shape), _const_spec(lv.shape)],
        out_specs=out_spec,
        out_shape=out_shape,
        scratch_shapes=[pltpu.VMEM((g, N_HEADS, DV, GLA_DK), F32)],
        compiler_params=_cparams(("arbitrary", "arbitrary")),
    )(hcat, hcat, hcat, hcat, hcat, w2, b2, nw, cum, lv)


def _hgrn_branch(hcat, llb, l1m, oml, nw, cum, lv):
    bsz, seq, _ = hcat.shape
    c = _tile(seq, GATED_BLOCK)
    g = _seq_group(bsz)
    out_spec, out_shape = _seq_out(g, c, bsz, seq)
    return pl.pallas_call(
        functools.partial(_hgrn_kernel, c=c),
        grid=(bsz // g, seq // c),
        in_specs=[_col_spec(g, c, 512, "hq"), _col_spec(g, c, 512, "hf"),
                  _col_spec(g, c, 512, "hi"), _col_spec(g, c, 512, "hg"),
                  _row_spec(512), _row_spec(512), _row_spec(512), _row_spec(512),
                  _const_spec(cum.shape), _const_spec(lv.shape)],
        out_specs=out_spec,
        out_shape=out_shape,
        scratch_shapes=[pltpu.VMEM((g, N_HEADS, DV, HGRN_DK), F32)],
        compiler_params=_cparams(("arbitrary", "arbitrary")),
    )(hcat, hcat, hcat, hcat, llb, l1m, oml, nw, cum, lv)


def _rope_kernel(pos_ref, invf_ref, cos_ref, sin_ref):
    ang = pos_ref[...].astype(F32) * invf_ref[...]
    cos_ref[...] = jnp.cos(ang)
    sin_ref[...] = jnp.sin(ang)


def _rope_tables(pos_col, invf):
    n = pos_col.shape[0]
    tm = _tile(n, 2048)
    w = invf.shape[1]
    return pl.pallas_call(
        _rope_kernel,
        grid=(n // tm,),
        in_specs=[pl.BlockSpec((tm, 1), lambda i: (i, 0)), pl.BlockSpec((1, w), lambda i: (0, 0))],
        out_specs=[pl.BlockSpec((tm, w), lambda i: (i, 0))] * 2,
        out_shape=[jax.ShapeDtypeStruct((n, w), F32)] * 2,
        compiler_params=_cparams(("arbitrary",)),
    )(pos_col, invf)


def _ret_consts(c):
    hh = np.arange(N_HEADS, dtype=np.float64)
    log_gamma = np.log(1.0 - 2.0 ** (-5.0 - hh))
    pos = np.arange(c, dtype=np.float64)
    diff = pos[:, None] - pos[None, :]
    dm = np.where(diff >= 0, np.exp(np.where(diff >= 0, diff, 0.0)[None] * log_gamma[:, None, None]), 0.0)
    col_head = np.arange(N_HEADS * RET_DK) // RET_DK
    qs = np.exp((pos + 1.0)[:, None] * log_gamma[col_head][None, :])
    ks = np.exp((c - 1.0 - pos)[:, None] * log_gamma[col_head][None, :])
    gc = np.exp(c * log_gamma[col_head])[None, :]
    return (jnp.asarray(dm, F32), jnp.asarray(qs, F32), jnp.asarray(ks, F32), jnp.asarray(gc, F32))


def _ret_kernel(q_ref, k_ref, qp_ref, kp_ref, v_ref, g_ref, cos_ref, sin_ref, dm_ref, qs_ref, ks_ref,
                gc_ref, o_ref, st_ref):
    @pl.when(pl.program_id(1) == 0)
    def _():
        st_ref[...] = jnp.zeros_like(st_ref)

    gc = gc_ref[...]
    states = _load_states(st_ref)
    results = []
    for bi in range(q_ref.shape[0]):
        outs = []
        new_states = []
        cs = cos_ref[bi]
        sn = sin_ref[bi]
        q = q_ref[bi] * cs + qp_ref[bi] * sn
        k = (k_ref[bi] * cs + kp_ref[bi] * sn) * (RET_DK ** -0.5)
        qb = q.astype(BF16)
        kb = k.astype(BF16)
        q_dec = (q * qs_ref[...]).astype(BF16)
        k_dec = (k * ks_ref[...]).astype(BF16)
        for h in range(N_HEADS):
            sk = slice(h * RET_DK, (h + 1) * RET_DK)
            sv = slice(h * DV, (h + 1) * DV)
            s = _dot_nt(qb[:, sk], kb[:, sk]) * dm_ref[h]
            vh = v_ref[bi, :, sv].astype(BF16)
            st = states[bi][h]
            o = _dot(s.astype(BF16), vh) + _dot_nt(q_dec[:, sk], st.astype(BF16))
            new_states.append(st * gc[:, sk] + _dot_tn(vh, k_dec[:, sk]))
            ms = jnp.mean(o * o, axis=-1, keepdims=True)
            gh = g_ref[bi, :, sv]
            outs.append((o * lax.rsqrt(ms + NORM_EPS) * (gh * _sigmoid(gh))).astype(BF16))
        results.append((outs, new_states))
    _store_results(results, o_ref, st_ref)


def _ret_branch(hcat, cos_t, sin_t, consts, c):
    bsz, seq, _ = hcat.shape
    g = _seq_group(bsz)
    out_spec, out_shape = _seq_out(g, c, bsz, seq)
    dm, qs, ks, gc = consts
    tab = pl.BlockSpec((g, c, 256), lambda b, j: (b, j, 0))
    return pl.pallas_call(
        _ret_kernel,
        grid=(bsz // g, seq // c),
        in_specs=[_col_spec(g, c, 256, "rq"), _col_spec(g, c, 256, "rk"),
                  _col_spec(g, c, 256, "rqp"), _col_spec(g, c, 256, "rkp"),
                  _col_spec(g, c, 512, "rv"), _col_spec(g, c, 512, "rg"), tab, tab,
                  _const_spec(dm.shape), _const_spec(qs.shape), _const_spec(ks.shape), _row_spec(256)],
        out_specs=out_spec,
        out_shape=out_shape,
        scratch_shapes=[pltpu.VMEM((g, N_HEADS, DV, RET_DK), F32)],
        compiler_params=_cparams(("arbitrary", "arbitrary")),
    )(hcat, hcat, hcat, hcat, hcat, hcat, cos_t, sin_t, dm, qs, ks, gc)


def _lru_kernel(x_ref, y_ref, cw_ref, cb_ref, wa_ref, ba_ref, wx_ref, bx_ref, lam_ref, o_ref,
                halo_ref, h_ref, *, c):
    @pl.when(pl.program_id(1) == 0)
    def _():
        halo_ref[...] = jnp.zeros_like(halo_ref)
        h_ref[...] = jnp.zeros_like(h_ref)

    cw = cw_ref[...]
    sp = _softplus(-lam_ref[...])
    rows = lax.broadcasted_iota(jnp.int32, (c, 1), 0)
    blk = BRANCH_WIDTH // 4
    for bi in range(x_ref.shape[0]):
        x = x_ref[bi]
        x_ext = jnp.concatenate([halo_ref[bi], x], axis=0)
        xc = cw[CONV_WIDTH - 1:CONV_WIDTH, :] * x
        for w in range(CONV_WIDTH - 1):
            shift = CONV_WIDTH - 1 - w
            xc = xc + cw[w:w + 1, :] * pltpu.roll(x_ext, shift, 0)[SUBLANES:, :]
        xc = xc + cb_ref[...]
        halo_ref[bi] = x[c - SUBLANES:, :]

        ra = []
        rx = []
        for g in range(4):
            xg = xc[:, g * blk:(g + 1) * blk].astype(BF16)
            ra.append(_dot(xg, wa_ref[g]))
            rx.append(_dot(xg, wx_ref[g]))
        r = _sigmoid(jnp.concatenate(ra, axis=1) + ba_ref[...])
        ig = _sigmoid(jnp.concatenate(rx, axis=1) + bx_ref[...])
        log_a = -LRU_C * r * sp
        a = jnp.exp(log_a)
        u = jnp.sqrt(-jnp.tanh(log_a) * (a * a + 1.0)) * (ig * xc)

        s = 1
        while s < c:
            keep = rows >= s
            a_sh = jnp.where(keep, pltpu.roll(a, s, 0), 1.0)
            u_sh = jnp.where(keep, pltpu.roll(u, s, 0), 0.0)
            u = a * u_sh + u
            a = a * a_sh
            s *= 2
        hs = u + a * h_ref[bi, 0:1, :]
        h_ref[bi] = jnp.broadcast_to(hs[c - 1:c, :], (SUBLANES, BRANCH_WIDTH))
        y = y_ref[bi]
        gelu = 0.5 * y * (1.0 + jnp.tanh(math.sqrt(2.0 / math.pi) * (y + 0.044715 * (y * y * y))))
        o_ref[bi] = (hs * gelu).astype(o_ref.dtype)


def _lru_branch(hcat, cw, cb, wa, ba, wx, bx, lam):
    bsz, seq, _ = hcat.shape
    c = _tile(seq, LRU_BLOCK)
    g = _seq_group(bsz)
    out_spec, out_shape = _seq_out(g, c, bsz, seq)
    return pl.pallas_call(
        functools.partial(_lru_kernel, c=c),
        grid=(bsz // g, seq // c),
        in_specs=[_col_spec(g, c, 512, "lx"), _col_spec(g, c, 512, "ly"),
                  _const_spec(cw.shape), _row_spec(512), _const_spec(wa.shape), _row_spec(512),
                  _const_spec(wx.shape), _row_spec(512), _row_spec(512)],
        out_specs=out_spec,
        out_shape=out_shape,
        scratch_shapes=[pltpu.VMEM((g, SUBLANES, BRANCH_WIDTH), F32), pltpu.VMEM((g, SUBLANES, BRANCH_WIDTH), F32)],
        compiler_params=_cparams(("arbitrary", "arbitrary")),
    )(hcat, hcat, cw, cb, wa, ba, wx, bx, lam)


def _merge_kernel(x_ref, o0_ref, o1_ref, o2_ref, o3_ref, wg_ref, wu_ref, b_ref, out_ref):
    x = x_ref[...]
    acc = None
    for br, o_ref in enumerate((o0_ref, o1_ref, o2_ref, o3_ref)):
        gate = _sigmoid(_dot(x, wg_ref[br]) + b_ref[br])
        term = gate * _dot(o_ref[...], wu_ref[br])
        acc = term if acc is None else acc + term
    out_ref[...] = acc.astype(out_ref.dtype)


def _merge(xb, outs, wg_all, wu_all, b_all, layer):
    n = xb.shape[0]
    tm = _tile(n, 1024)
    tn = 512
    ospec = pl.BlockSpec((tm, BRANCH_WIDTH), lambda i, j: (i, 0))
    return pl.pallas_call(
        _merge_kernel,
        grid=(n // tm, D_MODEL // tn),
        in_specs=[pl.BlockSpec((tm, D_MODEL), lambda i, j: (i, 0)), ospec, ospec, ospec, ospec,
                  pl.BlockSpec((None, 4, D_MODEL, tn), lambda i, j: (layer, 0, 0, j)),
                  pl.BlockSpec((None, 4, BRANCH_WIDTH, tn), lambda i, j: (layer, 0, 0, j)),
                  pl.BlockSpec((None, 4, 1, tn), lambda i, j: (layer, 0, 0, j))],
        out_specs=pl.BlockSpec((tm, tn), lambda i, j: (i, j)),
        out_shape=jax.ShapeDtypeStruct((n, D_MODEL), BF16),
        compiler_params=_cparams(("arbitrary", "arbitrary")),
    )(xb, *outs, wg_all, wu_all, b_all)


_HI16 = 0xFFFF0000


def _pack_bf16_pairs(val):
    half = val.shape[1] // 2
    lo = lax.bitcast_convert_type(val[:, :half].astype(BF16).astype(F32), jnp.uint32)
    hi = lax.bitcast_convert_type(val[:, half:].astype(BF16).astype(F32), jnp.uint32)
    return (lo >> 16) | (hi & jnp.uint32(_HI16))


def _unpack_bf16_pairs(words):
    lo = lax.bitcast_convert_type(words << 16, F32)
    hi = lax.bitcast_convert_type(words & jnp.uint32(_HI16), F32)
    return jnp.concatenate([lo, hi], axis=1)


def _to_rows(ref, val):
    m = val.shape[0]
    words = _pack_bf16_pairs(val)
    for j in range(ROW_CHUNKS):
        ref[pl.ds(j, m, stride=ROW_PITCH), :] = words[:, j * LANES:(j + 1) * LANES]
    for j in range(ROW_CHUNKS, ROW_PITCH):
        ref[pl.ds(j, m, stride=ROW_PITCH), :] = jnp.zeros((m, LANES), jnp.uint32)


def _from_rows(ref, m):
    words = jnp.concatenate([ref[pl.ds(j, m, stride=ROW_PITCH), :] for j in range(ROW_CHUNKS)], axis=1)
    return _unpack_bf16_pairs(words)


def _outln_kernel(m_ref, w_ref, x_ref, g_ref, b_ref, wr_ref, x1_ref, x1r_ref, lg_ref, *, alpha):
    y = alpha * x_ref[...] + _dot(m_ref[...], w_ref[...])
    x1 = _layer_norm(y, g_ref[...], b_ref[...])
    x1_ref[...] = x1
    _to_rows(x1r_ref, x1)
    lg_ref[...] = _dot_nt(wr_ref[...], x1.astype(BF16))


def _out_proj_ln(merged, w_all, x, g_all, b_all, wr_t, layer, alpha):
    n = x.shape[0]
    tm = _tile(n, 512)
    row = pl.BlockSpec((tm, D_MODEL), lambda i: (i, 0))
    vec = pl.BlockSpec((None, 1, D_MODEL), lambda i: (layer, 0, 0))
    return pl.pallas_call(
        functools.partial(_outln_kernel, alpha=alpha),
        grid=(n // tm,),
        in_specs=[row, pl.BlockSpec((None, D_MODEL, D_MODEL), lambda i: (layer, 0, 0)), row, vec, vec,
                  pl.BlockSpec((N_EXPERTS, D_MODEL), lambda i: (0, 0))],
        out_specs=[row, pl.BlockSpec((tm * ROW_PITCH, LANES), lambda i: (i, 0)),
                   pl.BlockSpec((N_EXPERTS, tm), lambda i: (0, i))],
        out_shape=[jax.ShapeDtypeStruct((n, D_MODEL), F32), jax.ShapeDtypeStruct((n * ROW_PITCH, LANES), jnp.uint32),
                   jax.ShapeDtypeStruct((N_EXPERTS, n), F32)],
        compiler_params=_cparams(("arbitrary",)),
    )(merged, w_all, x, g_all, b_all, wr_t)


def _route_kernel(lg_ref, bias_ref, idx_ref, wt_ref):
    s = _sigmoid(lg_ref[...])
    sel = s + bias_ref[...]
    tn = s.shape[1]
    rows = lax.broadcasted_iota(jnp.int32, (EXPERTS_PER_GROUP, tn), 0)
    neg = jnp.float32(-jnp.inf)

    def first_max(blk):
        m = jnp.max(blk, axis=0, keepdims=True)
        i = jnp.min(jnp.where(blk == m, rows, EXPERTS_PER_GROUP), axis=0, keepdims=True)
        return m, i

    best = None
    for g in range(N_GROUPS):
        sl = slice(g * EXPERTS_PER_GROUP, (g + 1) * EXPERTS_PER_GROUP)
        blk = sel[sl]
        sg = s[sl]
        m1, i1 = first_max(blk)
        m2, i2 = first_max(jnp.where(rows == i1, neg, blk))
        w1 = jnp.sum(jnp.where(rows == i1, sg, 0.0), axis=0, keepdims=True)
        w2 = jnp.sum(jnp.where(rows == i2, sg, 0.0), axis=0, keepdims=True)
        cand = (m1 + m2, i1 + g * EXPERTS_PER_GROUP, i2 + g * EXPERTS_PER_GROUP, w1, w2)
        if best is None:
            best = cand
        else:
            take = cand[0] > best[0]
            best = tuple(jnp.where(take, cn, bs) for cn, bs in zip(cand, best))
    _, e1, e2, w1, w2 = best
    tot = w1 + w2
    idx_ref[0:1, :] = e1
    idx_ref[1:2, :] = e2
    wt_ref[0:1, :] = w1 / tot
    wt_ref[1:2, :] = w2 / tot


def _route(logits_t, bias_col):
    n = logits_t.shape[1]
    tn = _tile(n, 2048)
    return pl.pallas_call(
        _route_kernel,
        grid=(n // tn,),
        in_specs=[pl.BlockSpec((N_EXPERTS, tn), lambda i: (0, i)),
                  pl.BlockSpec((N_EXPERTS, 1), lambda i: (0, 0))],
        out_specs=[pl.BlockSpec((2, tn), lambda i: (0, i))] * 2,
        out_shape=[jax.ShapeDtypeStruct((2, n), jnp.int32), jax.ShapeDtypeStruct((2, n), F32)],
        compiler_params=_cparams(("arbitrary",)),
    )(logits_t, bias_col)


def _rank_kernel(e_ref, u_ref, rank_ref, cnt_ref, carry_ref):
    @pl.when(pl.program_id(0) == 0)
    def _():
        carry_ref[...] = jnp.zeros_like(carry_ref)

    e = e_ref[...]
    tn = e.shape[1]
    rows = lax.broadcasted_iota(jnp.int32, (N_EXPERTS, tn), 0)
    onehot = jnp.where(rows == e, 1.0, 0.0)
    before = _dot(onehot.astype(BF16), u_ref[...])
    carry = carry_ref[...]
    rank_ref[...] = jnp.sum(onehot * (before + carry), axis=0, keepdims=True).astype(jnp.int32)
    carry = carry + jnp.sum(onehot, axis=1, keepdims=True)
    carry_ref[...] = carry
    cnt_ref[...] = carry


def _rank(e_row):
    m = e_row.shape[1]
    tn = _tile(m, RANK_TILE)
    upper = jnp.asarray(np.triu(np.ones((tn, tn), np.float32), 1), BF16)
    return pl.pallas_call(
        _rank_kernel,
        grid=(m // tn,),
        in_specs=[pl.BlockSpec((1, tn), lambda i: (0, i)), pl.BlockSpec((tn, tn), lambda i: (0, 0))],
        out_specs=[pl.BlockSpec((1, tn), lambda i: (0, i)), pl.BlockSpec((N_EXPERTS, 1), lambda i: (0, 0))],
        out_shape=[jax.ShapeDtypeStruct((1, m), jnp.int32), jax.ShapeDtypeStruct((N_EXPERTS, 1), F32)],
        scratch_shapes=[pltpu.VMEM((N_EXPERTS, 1), F32)],
        compiler_params=_cparams(("arbitrary",)),
    )(e_row, upper)


def _dispatch_plan(idx_t, n):
    m = 2 * n
    bm = MOE_BLOCK
    e_row = idx_t.reshape(1, m)
    rank, cnt = _rank(e_row)
    counts = cnt.reshape(N_EXPERTS).astype(jnp.int32)
    padded = (counts + bm - 1) // bm * bm
    pad_end = jnp.cumsum(padded)
    pad_start = pad_end - padded
    experts = jnp.arange(N_EXPERTS, dtype=jnp.int32)
    start_of = jnp.sum(jnp.where(e_row == experts[:, None], pad_start[:, None], 0), axis=0)
    dest = (rank.reshape(m) + start_of).astype(jnp.int32)
    n_rows = (m + N_EXPERTS * (bm - 1) + bm - 1) // bm * bm
    n_blocks = n_rows // bm
    starts = jnp.arange(n_blocks, dtype=jnp.int32) * bm
    block_e = jnp.minimum(jnp.sum((pad_end[None, :] <= starts[:, None]).astype(jnp.int32), axis=1),
                          N_EXPERTS - 1).astype(jnp.int32)
    n_used = (pad_end[-1] // bm).astype(jnp.int32).reshape(1)
    return dest, block_e, n_used, n_rows


def _scatter_kernel(dest_ref, x_ref, init_ref, out_ref, sem, *, tm, n):
    del init_ref
    base = pl.program_id(0) * tm

    def row_copy(r, k):
        return pltpu.make_async_copy(x_ref.at[pl.ds(r * ROW_PITCH, ROW_CHUNKS)],
                                     out_ref.at[pl.ds(dest_ref[k * n + base + r] * ROW_PITCH, ROW_CHUNKS)], sem)

    def start(r, carry):
        row_copy(r, 0).start(priority=0)
        row_copy(r, 1).start(priority=1)
        return carry

    lax.fori_loop(0, tm, start, 0, unroll=DMA_UNROLL)
    for _ in range(2):
        pltpu.make_async_copy(x_ref.at[pl.ds(0, tm * ROW_CHUNKS)], out_ref.at[pl.ds(0, tm * ROW_CHUNKS)], sem).wait()


def _scatter_rows(x_rows, dest, n_rows):
    n = x_rows.shape[0] // ROW_PITCH
    tm = _tile(n, SCATTER_TILE)
    init = jnp.zeros((n_rows * ROW_PITCH, LANES), x_rows.dtype)
    return pl.pallas_call(
        functools.partial(_scatter_kernel, tm=tm, n=n),
        grid_spec=pltpu.PrefetchScalarGridSpec(
            num_scalar_prefetch=1,
            grid=(n // tm,),
            in_specs=[pl.BlockSpec((tm * ROW_PITCH, LANES), lambda i, d: (i, 0)),
                      pl.BlockSpec(memory_space=pl.ANY)],
            out_specs=pl.BlockSpec(memory_space=pl.ANY),
            scratch_shapes=[pltpu.SemaphoreType.DMA],
        ),
        out_shape=jax.ShapeDtypeStruct((n_rows * ROW_PITCH, LANES), x_rows.dtype),
        input_output_aliases={2: 0},
        compiler_params=_cparams(("arbitrary",)),
    )(dest, x_rows, init)


def _moe_kernel(be_ref, nb_ref, x_ref, wgt_ref, wut_ref, wd_ref, y_ref):
    i = pl.program_id(0)

    @pl.when(i < nb_ref[0])
    def _():
        x = _from_rows(x_ref, MOE_BLOCK).astype(BF16)
        kc = D_MODEL // 4
        g = None
        u = None
        for j in range(4):
            xs = x[:, j * kc:(j + 1) * kc]
            gj = _dot_nt(xs, wgt_ref[:, j * kc:(j + 1) * kc].astype(BF16))
            uj = _dot_nt(xs, wut_ref[:, j * kc:(j + 1) * kc].astype(BF16))
            g = gj if g is None else g + gj
            u = uj if u is None else u + uj
        hid = ((g * _sigmoid(g)) * u).astype(BF16)
        _to_rows(y_ref, _dot(hid, wd_ref[...].astype(BF16)))

    @pl.when(i >= nb_ref[0])
    def _():
        y_ref[...] = jnp.zeros_like(y_ref)


def _moe_ffn(xs, block_e, n_used, wgt_all, wut_all, wd_all, layer):
    n_rows = xs.shape[0] // ROW_PITCH
    bm = MOE_BLOCK
    rows = pl.BlockSpec((bm * ROW_PITCH, LANES), lambda i, be, nb: (i, 0))
    wspec = pl.BlockSpec((None, None, D_EXPERT, D_MODEL), lambda i, be, nb: (layer, be[i], 0, 0))
    return pl.pallas_call(
        _moe_kernel,
        grid_spec=pltpu.PrefetchScalarGridSpec(
            num_scalar_prefetch=2,
            grid=(n_rows // bm,),
            in_specs=[rows, wspec, wspec, wspec],
            out_specs=rows,
        ),
        out_shape=jax.ShapeDtypeStruct((n_rows * ROW_PITCH, LANES), jnp.uint32),
        compiler_params=_cparams(("arbitrary",)),
    )(block_e, n_used, xs, wgt_all, wut_all, wd_all)


def _combine_kernel(dest_ref, x_ref, w_ref, yb_ref, g_ref, b_ref, o_ref, ob_ref, buf_ref, sem, *, tm, n, alpha):
    base = pl.program_id(0) * tm

    def row_copy(r, k):
        return pltpu.make_async_copy(yb_ref.at[pl.ds(dest_ref[k * n + base + r] * ROW_PITCH, ROW_CHUNKS)],
                                     buf_ref.at[k, pl.ds(r * ROW_PITCH, ROW_CHUNKS)], sem)

    def start(r, carry):
        row_copy(r, 0).start(priority=0)
        row_copy(r, 1).start(priority=1)
        return carry

    lax.fori_loop(0, tm, start, 0, unroll=DMA_UNROLL)
    for k in range(2):
        pltpu.make_async_copy(yb_ref.at[pl.ds(0, tm * ROW_CHUNKS)], buf_ref.at[k, pl.ds(0, tm * ROW_CHUNKS)],
                              sem).wait()
    w = w_ref[...]
    y = w[:, 0:1] * _from_rows(buf_ref.at[0], tm) + w[:, 1:2] * _from_rows(buf_ref.at[1], tm)
    x2 = _layer_norm(alpha * x_ref[...] + y, g_ref[...], b_ref[...])
    o_ref[...] = x2
    ob_ref[...] = x2.astype(BF16)


def _combine_ln(x1, w_col, yb, dest, g_all, b_all, layer, alpha):
    n = x1.shape[0]
    tm = _tile(n, COMBINE_TILE)
    row = pl.BlockSpec((tm, D_MODEL), lambda i, d: (i, 0))
    vec = pl.BlockSpec((None, 1, D_MODEL), lambda i, d: (layer, 0, 0))
    return pl.pallas_call(
        functools.partial(_combine_kernel, tm=tm, n=n, alpha=alpha),
        grid_spec=pltpu.PrefetchScalarGridSpec(
            num_scalar_prefetch=1,
            grid=(n // tm,),
            in_specs=[row, pl.BlockSpec((tm, 2), lambda i, d: (i, 0)), pl.BlockSpec(memory_space=pl.ANY), vec, vec],
            out_specs=[row, row],
            scratch_shapes=[pltpu.VMEM((2, tm * ROW_PITCH, LANES), jnp.uint32), pltpu.SemaphoreType.DMA],
        ),
        out_shape=[jax.ShapeDtypeStruct((n, D_MODEL), F32), jax.ShapeDtypeStruct((n, D_MODEL), BF16)],
        compiler_params=_cparams(("arbitrary",)),
    )(dest, x1, w_col, yb, g_all, b_all)


def _relayout_w_in(w_in):
    def seg(name):
        a, b = _ORIG[name]
        return w_in[..., a:b]

    def partner(name):
        a0 = _ORIG[name][0]
        half = RET_DK // 2
        parts = []
        for h in range(N_HEADS):
            b = a0 + RET_DK * h
            parts += [-w_in[..., b + half:b + RET_DK], w_in[..., b:b + half]]
        return parts

    lead = w_in.shape[:-1]
    cols = ([seg(nm) for nm in _SEG512] + [seg("gq"), seg("gk"), seg("rq"), seg("rk")]
            + partner("rq") + partner("rk")
            + [seg("glr"), jnp.zeros(lead + (LANES - GLA_RANK + _H_PAD,), w_in.dtype)])
    return jnp.concatenate(cols, axis=-1).astype(BF16)


def kernel(x, positions, w_in, w_gla_gate2, b_gla_gate2, gla_norm, hgrn_lb, hgrn_norm, conv_w, conv_b,
           w_lru_a, b_lru_a, w_lru_x, b_lru_x, lru_lambda, w_branch_up, w_merge_gate, b_merge_gate, w_out,
           ln1_g, ln1_b, w_router, router_bias, w_exp_gate, w_exp_up, w_exp_down, ln2_g, ln2_b):
    bsz, seq, d = x.shape
    depth = w_in.shape[0]
    n = bsz * seq
    alpha = (2 * depth) ** 0.25

    w_in_b = _relayout_w_in(w_in)
    w2_b = jnp.pad(w_gla_gate2, ((0, 0), (0, LANES - GLA_RANK), (0, 0))).astype(BF16)
    b2 = b_gla_gate2.reshape(depth, 1, -1)
    gla_nw = jnp.tile(gla_norm, (1, N_HEADS)).reshape(depth, 1, -1)
    hgrn_nw = jnp.tile(hgrn_norm, (1, N_HEADS)).reshape(depth, 1, -1)
    lb_cum = jnp.cumsum(jax.nn.softmax(hgrn_lb.astype(F32), axis=0), axis=0)
    lower = lb_cum - lb_cum[:1]
    log_lb = jnp.log(lower).reshape(depth, 1, -1)
    log_1m_lb = jnp.log1p(-lower).reshape(depth, 1, -1)
    one_m_lb = (1.0 - lower).reshape(depth, 1, -1)
    wa_b = w_lru_a.astype(BF16)
    wx_b = w_lru_x.astype(BF16)
    wg_b = w_merge_gate.astype(BF16)
    wu_b = w_branch_up.astype(BF16)
    bmg = b_merge_gate.reshape(depth, 4, 1, d)
    w_out_b = w_out.astype(BF16)
    wr_t = w_router.T.astype(BF16)
    bias_col = router_bias.reshape(N_EXPERTS, 1).astype(F32)
    ln1g = ln1_g.reshape(depth, 1, d)
    ln1b = ln1_b.reshape(depth, 1, d)
    ln2g = ln2_g.reshape(depth, 1, d)
    ln2b = ln2_b.reshape(depth, 1, d)
    wgt = jnp.swapaxes(w_exp_gate, 2, 3)
    wut = jnp.swapaxes(w_exp_up, 2, 3)

    half = RET_DK // 2
    inv_freq = ROPE_BASE ** (-jnp.arange(half, dtype=F32) / half)
    invf = jnp.tile(inv_freq, 2 * N_HEADS).reshape(1, N_HEADS * RET_DK)
    cos_t, sin_t = _rope_tables(positions.reshape(n, 1), invf)
    cos_t = cos_t.reshape(bsz, seq, -1)
    sin_t = sin_t.reshape(bsz, seq, -1)

    cum, lv = _gated_consts(_tile(seq, GATED_BLOCK))
    ret_c = _tile(seq, RET_BLOCK)
    ret_consts = _ret_consts(ret_c)

    x2d = x.reshape(n, d)
    xb = x2d.astype(BF16)
    for l in range(depth):
        hcat = _in_proj(xb, w_in_b, l).reshape(bsz, seq, H_COLS)
        o_gla = _gla_branch(hcat, w2_b[l], b2[l], gla_nw[l], cum, lv)
        o_hgrn = _hgrn_branch(hcat, log_lb[l], log_1m_lb[l], one_m_lb[l], hgrn_nw[l], cum, lv)
        o_ret = _ret_branch(hcat, cos_t, sin_t, ret_consts, ret_c)
        o_lru = _lru_branch(hcat, conv_w[l], conv_b[l].reshape(1, -1), wa_b[l], b_lru_a[l].reshape(1, -1),
                            wx_b[l], b_lru_x[l].reshape(1, -1), lru_lambda[l].reshape(1, -1))
        outs = tuple(o.reshape(n, BRANCH_WIDTH) for o in (o_gla, o_hgrn, o_ret, o_lru))
        merged = _merge(xb, outs, wg_b, wu_b, bmg, l)
        x1, x1_rows, logits_t = _out_proj_ln(merged, w_out_b, x2d, ln1g, ln1b, wr_t, l, alpha)
        idx_t, wt_t = _route(logits_t, bias_col)
        dest, block_e, n_used, n_rows = _dispatch_plan(idx_t, n)
        xs = _scatter_rows(x1_rows, dest, n_rows)
        yb = _moe_ffn(xs, block_e, n_used, wgt, wut, w_exp_down, l)
        x2d, xb = _combine_ln(x1, wt_t.T, yb, dest, ln2g, ln2b, l, alpha)
    return x2d.reshape(bsz, seq, d)
```

```python
import functools
import math

import numpy as np
import jax
import jax.numpy as jnp
from jax import lax
from jax.experimental import pallas as pl
from jax.experimental.pallas import tpu as pltpu

F32 = jnp.float32
BF16 = jnp.bfloat16

D_MODEL = 2048
N_HEADS = 4
DV = 128
BRANCH_WIDTH = 512
GLA_DK = 64
GLA_RANK = 16
GLA_GATE_NORM = 16.0
HGRN_DK = 128
RET_DK = 64
ROPE_BASE = 10000.0
CONV_WIDTH = 4
LRU_C = 8.0
N_EXPERTS = 32
N_GROUPS = 4
EXPERTS_PER_GROUP = 8
D_EXPERT = 704
NORM_EPS = 1e-5

LANES = 128
SUBLANES = 8
ROW_CHUNKS = D_MODEL // (2 * LANES)
ROW_PITCH = ROW_CHUNKS + 1
VMEM_LIMIT = 56 * 1024 * 1024

_ORIG_SIZES = (("gq", 256), ("gk", 256), ("gv", 512), ("gg", 512), ("glr", 16),
               ("hq", 512), ("hf", 512), ("hi", 512), ("hg", 512),
               ("rq", 256), ("rk", 256), ("rv", 512), ("rg", 512), ("lx", 512), ("ly", 512))
_ORIG = {}
_o = 0
for _n, _w in _ORIG_SIZES:
    _ORIG[_n] = (_o, _o + _w)
    _o += _w

_SEG512 = ("gv", "gg", "hq", "hf", "hi", "hg", "rv", "rg", "lx", "ly")
_SEG256 = ("gq", "gk", "rq", "rk", "rqp", "rkp")
_OFF = {}
_o = 0
for _n in _SEG512:
    _OFF[_n] = _o
    _o += 512
for _n in _SEG256:
    _OFF[_n] = _o
    _o += 256
_OFF["glr"] = _o
_o += LANES
H_COLS = 6912
_H_PAD = H_COLS - _o

GATED_BLOCK = 128
RET_BLOCK = 256
LRU_BLOCK = 256
SEQ_GROUP = 4
DMA_UNROLL = 8
MOE_BLOCK = 256
SCATTER_TILE = 512
COMBINE_TILE = 256
RANK_TILE = 1024


def _tile(n, pref):
    return pref if n % pref == 0 else n


def _cparams(sem):
    return pltpu.CompilerParams(dimension_semantics=sem, vmem_limit_bytes=VMEM_LIMIT)


def _dot(a, b):
    return jnp.dot(a, b, preferred_element_type=F32)


def _dot_nt(a, b):
    return lax.dot_general(a, b, (((1,), (1,)), ((), ())), preferred_element_type=F32)


def _dot_tn(a, b):
    return lax.dot_general(a, b, (((0,), (0,)), ((), ())), preferred_element_type=F32)


def _sigmoid(x):
    return 1.0 / (1.0 + jnp.exp(-x))


def _log_sigmoid(x):
    return jnp.minimum(x, 0.0) - jnp.log(1.0 + jnp.exp(-jnp.abs(x)))


def _softplus(x):
    return jnp.maximum(x, 0.0) + jnp.log(1.0 + jnp.exp(-jnp.abs(x)))


def _layer_norm(y, g, b):
    mu = jnp.mean(y, axis=-1, keepdims=True)
    yc = y - mu
    var = jnp.mean(yc * yc, axis=-1, keepdims=True)
    return yc * lax.rsqrt(var + NORM_EPS) * g + b


def _mm_kernel(x_ref, wt_ref, o_ref):
    o_ref[...] = _dot_nt(x_ref[...], wt_ref[...]).astype(o_ref.dtype)


def _in_proj(xb, wt_all, layer):
    n, k = xb.shape
    cols = wt_all.shape[1]
    tm = _tile(n, 512)
    tn = 2304
    return pl.pallas_call(
        _mm_kernel,
        grid=(cols // tn, n // tm),
        in_specs=[pl.BlockSpec((tm, k), lambda j, i: (i, 0)),
                  pl.BlockSpec((None, tn, k), lambda j, i: (layer, j, 0))],
        out_specs=pl.BlockSpec((tm, tn), lambda j, i: (i, j)),
        out_shape=jax.ShapeDtypeStruct((n, cols), F32),
        compiler_params=_cparams(("arbitrary", "arbitrary")),
    )(xb, wt_all)


def _gated_consts(c):
    n16 = c // 16
    i = np.arange(c)
    g = i // 16
    hs = []
    h = 1
    while h <= n16:
        hs.append(h)
        h *= 2
    same16 = g[:, None] == g[None, :]
    mats = [same16 & (i[None, :] <= i[:, None]), same16 & (i[None, :] > i[:, None])]
    gq = np.arange(n16)[:, None]
    gt = g[None, :]
    for h in hs[1:]:
        lo = (gq // h) * h
        mats.append((gt >= lo) & (gt < gq))
        mats.append((gt > gq) & (gt <= lo + h - 1))
    cum = np.concatenate(mats, 0).astype(np.float32)
    lv = -np.ones((c, c), np.int32)
    lv[(g[:, None] == g[None, :]) & (i[None, :] <= i[:, None])] = 0
    gi = g[:, None]
    gj = g[None, :]
    for li, h in enumerate(hs[:-1]):
        m = (gi // (2 * h) == gj // (2 * h)) & ((gi // h) % 2 == 1) & ((gj // h) % 2 == 0)
        lv[m] = li + 1
    return jnp.asarray(cum, BF16), jnp.asarray(lv)


def _exact_select_dot(sel, x):
    hi = x.astype(BF16)
    r1 = x - hi.astype(F32)
    mid = r1.astype(BF16)
    lo = (r1 - mid.astype(F32)).astype(BF16)
    return _dot(sel, hi) + _dot(sel, mid) + _dot(sel, lo)


def _expand_groups(t):
    w = t.shape[1]
    return jnp.concatenate([jnp.broadcast_to(t[g:g + 1, :], (16, w)) for g in range(t.shape[0])], axis=0)


def _gated_core(q, k, la, v_ref, g_ref, nw_ref, cum_ref, lv_ref, states, *, c, dk):
    ex = _exact_select_dot(cum_ref[...], la)
    n16 = c // 16
    nl = 1 + (ex.shape[0] - 2 * c) // (2 * n16)
    c_t = [ex[0:c]]
    c_e = [ex[c:2 * c]]
    for l in range(1, nl):
        r0 = 2 * c + 2 * (l - 1) * n16
        c_t.append(c_t[0] + _expand_groups(ex[r0:r0 + n16]))
        c_e.append(c_e[0] + _expand_groups(ex[r0 + n16:r0 + 2 * n16]))
    qd = [(q * jnp.exp(c_t[l])).astype(BF16) for l in range(nl)]
    k_diag = (k * jnp.exp(-c_t[0])).astype(BF16)
    kd = [(k * jnp.exp(c_e[l])).astype(BF16) for l in range(nl)]
    dec = jnp.exp(c_t[nl - 1][c - 1:c, :])
    lv = lv_ref[...]
    outs = []
    new_states = []
    for h in range(N_HEADS):
        sk = slice(h * dk, (h + 1) * dk)
        sv = slice(h * DV, (h + 1) * DV)
        p = jnp.where(lv == 0, _dot_nt(qd[0][:, sk], k_diag[:, sk]), 0.0)
        for l in range(nl - 1):
            p = p + jnp.where(lv == l + 1, _dot_nt(qd[l][:, sk], kd[l][:, sk]), 0.0)
        vh = v_ref[:, sv].astype(BF16)
        st = states[h]
        o = _dot(p.astype(BF16), vh) + _dot_nt(qd[nl - 1][:, sk], st.astype(BF16))
        new_states.append(st * dec[:, sk] + _dot_tn(vh, kd[nl - 1][:, sk]))
        ms = jnp.mean(o * o, axis=-1, keepdims=True)
        gh = g_ref[:, sv]
        outs.append((o * lax.rsqrt(ms + NORM_EPS) * nw_ref[:, sv] * (gh * _sigmoid(gh))).astype(BF16))
    return outs, new_states


def _load_states(st_ref):
    return [[st_ref[bi, h] for h in range(N_HEADS)] for bi in range(st_ref.shape[0])]


def _store_results(results, o_ref, st_ref):
    for bi, (outs, new_states) in enumerate(results):
        for h in range(N_HEADS):
            o_ref[bi, :, h * DV:(h + 1) * DV] = outs[h]
            st_ref[bi, h] = new_states[h]


def _gla_kernel(q_ref, k_ref, v_ref, g_ref, lr_ref, w2_ref, b2_ref, nw_ref, cum_ref, lv_ref,
                o_ref, st_ref, *, c):
    @pl.when(pl.program_id(1) == 0)
    def _():
        st_ref[...] = jnp.zeros_like(st_ref)

    states = _load_states(st_ref)
    results = []
    for bi in range(q_ref.shape[0]):
        q = q_ref[bi] * (GLA_DK ** -0.5)
        logits = _dot(lr_ref[bi].astype(BF16), w2_ref[...]) + b2_ref[...]
        la = _log_sigmoid(logits) * (1.0 / GLA_GATE_NORM)
        results.append(_gated_core(q, k_ref[bi], la, v_ref.at[bi], g_ref.at[bi], nw_ref, cum_ref, lv_ref,
                                   states[bi], c=c, dk=GLA_DK))
    _store_results(results, o_ref, st_ref)


def _hgrn_kernel(q_ref, f_ref, v_ref, g_ref, llb_ref, l1m_ref, oml_ref, nw_ref, cum_ref, lv_ref,
                 o_ref, st_ref, *, c):
    @pl.when(pl.program_id(1) == 0)
    def _():
        st_ref[...] = jnp.zeros_like(st_ref)

    states = _load_states(st_ref)
    results = []
    for bi in range(q_ref.shape[0]):
        z = f_ref[bi]
        a = llb_ref[...]
        b = l1m_ref[...] + _log_sigmoid(z)
        la = jnp.maximum(a, b) + jnp.log(1.0 + jnp.exp(-jnp.abs(a - b)))
        k = oml_ref[...] * _sigmoid(-z)
        qq = q_ref[bi]
        q = qq * _sigmoid(qq)
        results.append(_gated_core(q, k, la, v_ref.at[bi], g_ref.at[bi], nw_ref, cum_ref, lv_ref,
                                   states[bi], c=c, dk=HGRN_DK))
    _store_results(results, o_ref, st_ref)


def _seq_group(bsz):
    for g in (SEQ_GROUP, 2):
        if bsz % g == 0:
            return g
    return 1


def _col_spec(g, c, width, name):
    cb = _OFF[name] // width
    return pl.BlockSpec((g, c, width), lambda b, j: (b, j, cb))


def _seq_out(g, c, bsz, seq):
    return (pl.BlockSpec((g, c, BRANCH_WIDTH), lambda b, j: (b, j, 0)),
            jax.ShapeDtypeStruct((bsz, seq, BRANCH_WIDTH), BF16))


def _row_spec(width):
    return pl.BlockSpec((1, width), lambda b, j: (0, 0))


def _const_spec(shape):
    nd = len(shape)
    return pl.BlockSpec(shape, lambda b, j: (0,) * nd)


def _gla_branch(hcat, w2, b2, nw, cum, lv):
    bsz, seq, _ = hcat.shape
    c = _tile(seq, GATED_BLOCK)
    g = _seq_group(bsz)
    out_spec, out_shape = _seq_out(g, c, bsz, seq)
    return pl.pallas_call(
        functools.partial(_gla_kernel, c=c),
        grid=(bsz // g, seq // c),
        in_specs=[_col_spec(g, c, 256, "gq"), _col_spec(g, c, 256, "gk"),
                  _col_spec(g, c, 512, "gv"), _col_spec(g, c, 512, "gg"),
                  _col_spec(g, c, LANES, "glr"),
                  _const_spec(w2.shape), _row_spec(256), _row_spec(512),
                  _const_spec(cum.shape), _const_spec(lv.shape)],
        out_specs=out_spec,
        out_shape=out_shape,
        scratch_shapes=[pltpu.VMEM((g, N_HEADS, DV, GLA_DK), F32)],
        compiler_params=_cparams(("arbitrary", "arbitrary")),
    )(hcat, hcat, hcat, hcat, hcat, w2, b2, nw, cum, lv)


def _hgrn_branch(hcat, llb, l1m, oml, nw, cum, lv):
    bsz, seq, _ = hcat.shape
    c = _tile(seq, GATED_BLOCK)
    g = _seq_group(bsz)
    out_spec, out_shape = _seq_out(g, c, bsz, seq)
    return pl.pallas_call(
        functools.partial(_hgrn_kernel, c=c),
        grid=(bsz // g, seq // c),
        in_specs=[_col_spec(g, c, 512, "hq"), _col_spec(g, c, 512, "hf"),
                  _col_spec(g, c, 512, "hi"), _col_spec(g, c, 512, "hg"),
                  _row_spec(512), _row_spec(512), _row_spec(512), _row_spec(512),
                  _const_spec(cum.shape), _const_spec(lv.shape)],
        out_specs=out_spec,
        out_shape=out_shape,
        scratch_shapes=[pltpu.VMEM((g, N_HEADS, DV, HGRN_DK), F32)],
        compiler_params=_cparams(("arbitrary", "arbitrary")),
    )(hcat, hcat, hcat, hcat, llb, l1m, oml, nw, cum, lv)


def _rope_kernel(pos_ref, invf_ref, cos_ref, sin_ref):
    ang = pos_ref[...].astype(F32) * invf_ref[...]
    cos_ref[...] = jnp.cos(ang)
    sin_ref[...] = jnp.sin(ang)


def _rope_tables(pos_col, invf):
    n = pos_col.shape[0]
    tm = _tile(n, 2048)
    w = invf.shape[1]
    return pl.pallas_call(
        _rope_kernel,
        grid=(n // tm,),
        in_specs=[pl.BlockSpec((tm, 1), lambda i: (i, 0)), pl.BlockSpec((1, w), lambda i: (0, 0))],
        out_specs=[pl.BlockSpec((tm, w), lambda i: (i, 0))] * 2,
        out_shape=[jax.ShapeDtypeStruct((n, w), F32)] * 2,
        compiler_params=_cparams(("arbitrary",)),
    )(pos_col, invf)


def _ret_consts(c):
    hh = np.arange(N_HEADS, dtype=np.float64)
    log_gamma = np.log(1.0 - 2.0 ** (-5.0 - hh))
    pos = np.arange(c, dtype=np.float64)
    diff = pos[:, None] - pos[None, :]
    dm = np.where(diff >= 0, np.exp(np.where(diff >= 0, diff, 0.0)[None] * log_gamma[:, None, None]), 0.0)
    col_head = np.arange(N_HEADS * RET_DK) // RET_DK
    qs = np.exp((pos + 1.0)[:, None] * log_gamma[col_head][None, :])
    ks = np.exp((c - 1.0 - pos)[:, None] * log_gamma[col_head][None, :])
    gc = np.exp(c * log_gamma[col_head])[None, :]
    return (jnp.asarray(dm, F32), jnp.asarray(qs, F32), jnp.asarray(ks, F32), jnp.asarray(gc, F32))


def _ret_kernel(q_ref, k_ref, qp_ref, kp_ref, v_ref, g_ref, cos_ref, sin_ref, dm_ref, qs_ref, ks_ref,
                gc_ref, o_ref, st_ref):
    @pl.when(pl.program_id(1) == 0)
    def _():
        st_ref[...] = jnp.zeros_like(st_ref)

    gc = gc_ref[...]
    states = _load_states(st_ref)
    results = []
    for bi in range(q_ref.shape[0]):
        outs = []
        new_states = []
        cs = cos_ref[bi]
        sn = sin_ref[bi]
        q = q_ref[bi] * cs + qp_ref[bi] * sn
        k = (k_ref[bi] * cs + kp_ref[bi] * sn) * (RET_DK ** -0.5)
        qb = q.astype(BF16)
        kb = k.astype(BF16)
        q_dec = (q * qs_ref[...]).astype(BF16)
        k_dec = (k * ks_ref[...]).astype(BF16)
        for h in range(N_HEADS):
            sk = slice(h * RET_DK, (h + 1) * RET_DK)
            sv = slice(h * DV, (h + 1) * DV)
            s = _dot_nt(qb[:, sk], kb[:, sk]) * dm_ref[h]
            vh = v_ref[bi, :, sv].astype(BF16)
            st = states[bi][h]
            o = _dot(s.astype(BF16), vh) + _dot_nt(q_dec[:, sk], st.astype(BF16))
            new_states.append(st * gc[:, sk] + _dot_tn(vh, k_dec[:, sk]))
            ms = jnp.mean(o * o, axis=-1, keepdims=True)
            gh = g_ref[bi, :, sv]
            outs.append((o * lax.rsqrt(ms + NORM_EPS) * (gh * _sigmoid(gh))).astype(BF16))
        results.append((outs, new_states))
    _store_results(results, o_ref, st_ref)


def _ret_branch(hcat, cos_t, sin_t, consts, c):
    bsz, seq, _ = hcat.shape
    g = _seq_group(bsz)
    out_spec, out_shape = _seq_out(g, c, bsz, seq)
    dm, qs, ks, gc = consts
    tab = pl.BlockSpec((g, c, 256), lambda b, j: (b, j, 0))
    return pl.pallas_call(
        _ret_kernel,
        grid=(bsz // g, seq // c),
        in_specs=[_col_spec(g, c, 256, "rq"), _col_spec(g, c, 256, "rk"),
                  _col_spec(g, c, 256, "rqp"), _col_spec(g, c, 256, "rkp"),
                  _col_spec(g, c, 512, "rv"), _col_spec(g, c, 512, "rg"), tab, tab,
                  _const_spec(dm.shape), _const_spec(qs.shape), _const_spec(ks.shape), _row_spec(256)],
        out_specs=out_spec,
        out_shape=out_shape,
        scratch_shapes=[pltpu.VMEM((g, N_HEADS, DV, RET_DK), F32)],
        compiler_params=_cparams(("arbitrary", "arbitrary")),
    )(hcat, hcat, hcat, hcat, hcat, hcat, cos_t, sin_t, dm, qs, ks, gc)


def _lru_kernel(x_ref, y_ref, cw_ref, cb_ref, wa_ref, ba_ref, wx_ref, bx_ref, lam_ref, o_ref,
                halo_ref, h_ref, *, c):
    @pl.when(pl.program_id(1) == 0)
    def _():
        halo_ref[...] = jnp.zeros_like(halo_ref)
        h_ref[...] = jnp.zeros_like(h_ref)

    cw = cw_ref[...]
    sp = _softplus(-lam_ref[...])
    rows = lax.broadcasted_iota(jnp.int32, (c, 1), 0)
    blk = BRANCH_WIDTH // 4
    for bi in range(x_ref.shape[0]):
        x = x_ref[bi]
        x_ext = jnp.concatenate([halo_ref[bi], x], axis=0)
        xc = cw[CONV_WIDTH - 1:CONV_WIDTH, :] * x
        for w in range(CONV_WIDTH - 1):
            shift = CONV_WIDTH - 1 - w
            xc = xc + cw[w:w + 1, :] * pltpu.roll(x_ext, shift, 0)[SUBLANES:, :]
        xc = xc + cb_ref[...]
        halo_ref[bi] = x[c - SUBLANES:, :]

        ra = []
        rx = []
        for g in range(4):
            xg = xc[:, g * blk:(g + 1) * blk].astype(BF16)
            ra.append(_dot(xg, wa_ref[g]))
            rx.append(_dot(xg, wx_ref[g]))
        r = _sigmoid(jnp.concatenate(ra, axis=1) + ba_ref[...])
        ig = _sigmoid(jnp.concatenate(rx, axis=1) + bx_ref[...])
        log_a = -LRU_C * r * sp
        a = jnp.exp(log_a)
        u = jnp.sqrt(-jnp.tanh(log_a) * (a * a + 1.0)) * (ig * xc)

        s = 1
        while s < c:
            keep = rows >= s
            a_sh = jnp.where(keep, pltpu.roll(a, s, 0), 1.0)
            u_sh = jnp.where(keep, pltpu.roll(u, s, 0), 0.0)
            u = a * u_sh + u
            a = a * a_sh
            s *= 2
        hs = u + a * h_ref[bi, 0:1, :]
        h_ref[bi] = jnp.broadcast_to(hs[c - 1:c, :], (SUBLANES, BRANCH_WIDTH))
        y = y_ref[bi]
        gelu = 0.5 * y * (1.0 + jnp.tanh(math.sqrt(2.0 / math.pi) * (y + 0.044715 * (y * y * y))))
        o_ref[bi] = (hs * gelu).astype(o_ref.dtype)


def _lru_branch(hcat, cw, cb, wa, ba, wx, bx, lam):
    bsz, seq, _ = hcat.shape
    c = _tile(seq, LRU_BLOCK)
    g = _seq_group(bsz)
    out_spec, out_shape = _seq_out(g, c, bsz, seq)
    return pl.pallas_call(
        functools.partial(_lru_kernel, c=c),
        grid=(bsz // g, seq // c),
        in_specs=[_col_spec(g, c, 512, "lx"), _col_spec(g, c, 512, "ly"),
                  _const_spec(cw.shape), _row_spec(512), _const_spec(wa.shape), _row_spec(512),
                  _const_spec(wx.shape), _row_spec(512), _row_spec(512)],
        out_specs=out_spec,
        out_shape=out_shape,
        scratch_shapes=[pltpu.VMEM((g, SUBLANES, BRANCH_WIDTH), F32), pltpu.VMEM((g, SUBLANES, BRANCH_WIDTH), F32)],
        compiler_params=_cparams(("arbitrary", "arbitrary")),
    )(hcat, hcat, cw, cb, wa, ba, wx, bx, lam)


def _merge_kernel(x_ref, o0_ref, o1_ref, o2_ref, o3_ref, wg_ref, wu_ref, b_ref, out_ref):
    x = x_ref[...]
    acc = None
    for br, o_ref in enumerate((o0_ref, o1_ref, o2_ref, o3_ref)):
        gate = _sigmoid(_dot(x, wg_ref[br]) + b_ref[br])
        term = gate * _dot(o_ref[...], wu_ref[br])
        acc = term if acc is None else acc + term
    out_ref[...] = acc.astype(out_ref.dtype)


def _merge(xb, outs, wg_all, wu_all, b_all, layer):
    n = xb.shape[0]
    tm = _tile(n, 1024)
    tn = 512
    ospec = pl.BlockSpec((tm, BRANCH_WIDTH), lambda i, j: (i, 0))
    return pl.pallas_call(
        _merge_kernel,
        grid=(n // tm, D_MODEL // tn),
        in_specs=[pl.BlockSpec((tm, D_MODEL), lambda i, j: (i, 0)), ospec, ospec, ospec, ospec,
                  pl.BlockSpec((None, 4, D_MODEL, tn), lambda i, j: (layer, 0, 0, j)),
                  pl.BlockSpec((None, 4, BRANCH_WIDTH, tn), lambda i, j: (layer, 0, 0, j)),
                  pl.BlockSpec((None, 4, 1, tn), lambda i, j: (layer, 0, 0, j))],
        out_specs=pl.BlockSpec((tm, tn), lambda i, j: (i, j)),
        out_shape=jax.ShapeDtypeStruct((n, D_MODEL), BF16),
        compiler_params=_cparams(("arbitrary", "arbitrary")),
    )(xb, *outs, wg_all, wu_all, b_all)


_HI16 = 0xFFFF0000


def _pack_bf16_pairs(val):
    half = val.shape[1] // 2
    lo = lax.bitcast_convert_type(val[:, :half].astype(BF16).astype(F32), jnp.uint32)
    hi = lax.bitcast_convert_type(val[:, half:].astype(BF16).astype(F32), jnp.uint32)
    return (lo >> 16) | (hi & jnp.uint32(_HI16))


def _unpack_bf16_pairs(words):
    lo = lax.bitcast_convert_type(words << 16, F32)
    hi = lax.bitcast_convert_type(words & jnp.uint32(_HI16), F32)
    return jnp.concatenate([lo, hi], axis=1)


def _to_rows(ref, val):
    m = val.shape[0]
    words = _pack_bf16_pairs(val)
    for j in range(ROW_CHUNKS):
        ref[pl.ds(j, m, stride=ROW_PITCH), :] = words[:, j * LANES:(j + 1) * LANES]
    for j in range(ROW_CHUNKS, ROW_PITCH):
        ref[pl.ds(j, m, stride=ROW_PITCH), :] = jnp.zeros((m, LANES), jnp.uint32)


def _from_rows(ref, m):
    words = jnp.concatenate([ref[pl.ds(j, m, stride=ROW_PITCH), :] for j in range(ROW_CHUNKS)], axis=1)
    return _unpack_bf16_pairs(words)


def _outln_kernel(m_ref, w_ref, x_ref, g_ref, b_ref, wr_ref, x1_ref, x1r_ref, lg_ref, *, alpha):
    y = alpha * x_ref[...] + _dot(m_ref[...], w_ref[...])
    x1 = _layer_norm(y, g_ref[...], b_ref[...])
    x1_ref[...] = x1
    _to_rows(x1r_ref, x1)
    lg_ref[...] = _dot_nt(wr_ref[...], x1.astype(BF16))


def _out_proj_ln(merged, w_all, x, g_all, b_all, wr_t, layer, alpha):
    n = x.shape[0]
    tm = _tile(n, 512)
    row = pl.BlockSpec((tm, D_MODEL), lambda i: (i, 0))
    vec = pl.BlockSpec((None, 1, D_MODEL), lambda i: (layer, 0, 0))
    return pl.pallas_call(
        functools.partial(_outln_kernel, alpha=alpha),
        grid=(n // tm,),
        in_specs=[row, pl.BlockSpec((None, D_MODEL, D_MODEL), lambda i: (layer, 0, 0)), row, vec, vec,
                  pl.BlockSpec((N_EXPERTS, D_MODEL), lambda i: (0, 0))],
        out_specs=[row, pl.BlockSpec((tm * ROW_PITCH, LANES), lambda i: (i, 0)),
                   pl.BlockSpec((N_EXPERTS, tm), lambda i: (0, i))],
        out_shape=[jax.ShapeDtypeStruct((n, D_MODEL), F32), jax.ShapeDtypeStruct((n * ROW_PITCH, LANES), jnp.uint32),
                   jax.ShapeDtypeStruct((N_EXPERTS, n), F32)],
        compiler_params=_cparams(("arbitrary",)),
    )(merged, w_all, x, g_all, b_all, wr_t)


def _route_kernel(lg_ref, bias_ref, idx_ref, wt_ref):
    s = _sigmoid(lg_ref[...])
    sel = s + bias_ref[...]
    tn = s.shape[1]
    rows = lax.broadcasted_iota(jnp.int32, (EXPERTS_PER_GROUP, tn), 0)
    neg = jnp.float32(-jnp.inf)

    def first_max(blk):
        m = jnp.max(blk, axis=0, keepdims=True)
        i = jnp.min(jnp.where(blk == m, rows, EXPERTS_PER_GROUP), axis=0, keepdims=True)
        return m, i

    best = None
    for g in range(N_GROUPS):
        sl = slice(g * EXPERTS_PER_GROUP, (g + 1) * EXPERTS_PER_GROUP)
        blk = sel[sl]
        sg = s[sl]
        m1, i1 = first_max(blk)
        m2, i2 = first_max(jnp.where(rows == i1, neg, blk))
        w1 = jnp.sum(jnp.where(rows == i1, sg, 0.0), axis=0, keepdims=True)
        w2 = jnp.sum(jnp.where(rows == i2, sg, 0.0), axis=0, keepdims=True)
        cand = (m1 + m2, i1 + g * EXPERTS_PER_GROUP, i2 + g * EXPERTS_PER_GROUP, w1, w2)
        if best is None:
            best = cand
        else:
            take = cand[0] > best[0]
            best = tuple(jnp.where(take, cn, bs) for cn, bs in zip(cand, best))
    _, e1, e2, w1, w2 = best
    tot = w1 + w2
    idx_ref[0:1, :] = e1
    idx_ref[1:2, :] = e2
    wt_ref[0:1, :] = w1 / tot
    wt_ref[1:2, :] = w2 / tot


def _route(logits_t, bias_col):
    n = logits_t.shape[1]
    tn = _tile(n, 2048)
    return pl.pallas_call(
        _route_kernel,
        grid=(n // tn,),
        in_specs=[pl.BlockSpec((N_EXPERTS, tn), lambda i: (0, i)),
                  pl.BlockSpec((N_EXPERTS, 1), lambda i: (0, 0))],
        out_specs=[pl.BlockSpec((2, tn), lambda i: (0, i))] * 2,
        out_shape=[jax.ShapeDtypeStruct((2, n), jnp.int32), jax.ShapeDtypeStruct((2, n), F32)],
        compiler_params=_cparams(("arbitrary",)),
    )(logits_t, bias_col)


def _rank_kernel(e_ref, u_ref, rank_ref, cnt_ref, carry_ref):
    @pl.when(pl.program_id(0) == 0)
    def _():
        carry_ref[...] = jnp.zeros_like(carry_ref)

    e = e_ref[...]
    tn = e.shape[1]
    rows = lax.broadcasted_iota(jnp.int32, (N_EXPERTS, tn), 0)
    onehot = jnp.where(rows == e, 1.0, 0.0)
    before = _dot(onehot.astype(BF16), u_ref[...])
    carry = carry_ref[...]
    rank_ref[...] = jnp.sum(onehot * (before + carry), axis=0, keepdims=True).astype(jnp.int32)
    carry = carry + jnp.sum(onehot, axis=1, keepdims=True)
    carry_ref[...] = carry
    cnt_ref[...] = carry


def _rank(e_row):
    m = e_row.shape[1]
    tn = _tile(m, RANK_TILE)
    upper = jnp.asarray(np.triu(np.ones((tn, tn), np.float32), 1), BF16)
    return pl.pallas_call(
        _rank_kernel,
        grid=(m // tn,),
        in_specs=[pl.BlockSpec((1, tn), lambda i: (0, i)), pl.BlockSpec((tn, tn), lambda i: (0, 0))],
        out_specs=[pl.BlockSpec((1, tn), lambda i: (0, i)), pl.BlockSpec((N_EXPERTS, 1), lambda i: (0, 0))],
        out_shape=[jax.ShapeDtypeStruct((1, m), jnp.int32), jax.ShapeDtypeStruct((N_EXPERTS, 1), F32)],
        scratch_shapes=[pltpu.VMEM((N_EXPERTS, 1), F32)],
        compiler_params=_cparams(("arbitrary",)),
    )(e_row, upper)


def _dispatch_plan(idx_t, n):
    m = 2 * n
    bm = MOE_BLOCK
    e_row = idx_t.reshape(1, m)
    rank, cnt = _rank(e_row)
    counts = cnt.reshape(N_EXPERTS).astype(jnp.int32)
    padded = (counts + bm - 1) // bm * bm
    pad_end = jnp.cumsum(padded)
    pad_start = pad_end - padded
    experts = jnp.arange(N_EXPERTS, dtype=jnp.int32)
    start_of = jnp.sum(jnp.where(e_row == experts[:, None], pad_start[:, None], 0), axis=0)
    dest = (rank.reshape(m) + start_of).astype(jnp.int32)
    n_rows = (m + N_EXPERTS * (bm - 1) + bm - 1) // bm * bm
    n_blocks = n_rows // bm
    starts = jnp.arange(n_blocks, dtype=jnp.int32) * bm
    block_e = jnp.minimum(jnp.sum((pad_end[None, :] <= starts[:, None]).astype(jnp.int32), axis=1),
                          N_EXPERTS - 1).astype(jnp.int32)
    n_used = (pad_end[-1] // bm).astype(jnp.int32).reshape(1)
    return dest, block_e, n_used, n_rows


def _scatter_kernel(dest_ref, x_ref, init_ref, out_ref, sem, *, tm, n):
    del init_ref
    base = pl.program_id(0) * tm

    def row_copy(r, k):
        return pltpu.make_async_copy(x_ref.at[pl.ds(r * ROW_PITCH, ROW_CHUNKS)],
                                     out_ref.at[pl.ds(dest_ref[k * n + base + r] * ROW_PITCH, ROW_CHUNKS)], sem)

    def start(r, carry):
        row_copy(r, 0).start(priority=0)
        row_copy(r, 1).start(priority=1)
        return carry

    lax.fori_loop(0, tm, start, 0, unroll=DMA_UNROLL)
    for _ in range(2):
        pltpu.make_async_copy(x_ref.at[pl.ds(0, tm * ROW_CHUNKS)], out_ref.at[pl.ds(0, tm * ROW_CHUNKS)], sem).wait()


def _scatter_rows(x_rows, dest, n_rows):
    n = x_rows.shape[0] // ROW_PITCH
    tm = _tile(n, SCATTER_TILE)
    init = jnp.zeros((n_rows * ROW_PITCH, LANES), x_rows.dtype)
    return pl.pallas_call(
        functools.partial(_scatter_kernel, tm=tm, n=n),
        grid_spec=pltpu.PrefetchScalarGridSpec(
            num_scalar_prefetch=1,
            grid=(n // tm,),
            in_specs=[pl.BlockSpec((tm * ROW_PITCH, LANES), lambda i, d: (i, 0)),
                      pl.BlockSpec(memory_space=pl.ANY)],
            out_specs=pl.BlockSpec(memory_space=pl.ANY),
            scratch_shapes=[pltpu.SemaphoreType.DMA],
        ),
        out_shape=jax.ShapeDtypeStruct((n_rows * ROW_PITCH, LANES), x_rows.dtype),
        input_output_aliases={2: 0},
        compiler_params=_cparams(("arbitrary",)),
    )(dest, x_rows, init)


def _moe_kernel(be_ref, nb_ref, x_ref, wgt_ref, wut_ref, wd_ref, y_ref):
    i = pl.program_id(0)

    @pl.when(i < nb_ref[0])
    def _():
        x = _from_rows(x_ref, MOE_BLOCK).astype(BF16)
        kc = D_MODEL // 4
        g = None
        u = None
        for j in range(4):
            xs = x[:, j * kc:(j + 1) * kc]
            gj = _dot_nt(xs, wgt_ref[:, j * kc:(j + 1) * kc].astype(BF16))
            uj = _dot_nt(xs, wut_ref[:, j * kc:(j + 1) * kc].astype(BF16))
            g = gj if g is None else g + gj
            u = uj if u is None else u + uj
        hid = ((g * _sigmoid(g)) * u).astype(BF16)
        _to_rows(y_ref, _dot(hid, wd_ref[...].astype(BF16)))

    @pl.when(i >= nb_ref[0])
    def _():
        y_ref[...] = jnp.zeros_like(y_ref)


def _moe_ffn(xs, block_e, n_used, wgt_all, wut_all, wd_all, layer):
    n_rows = xs.shape[0] // ROW_PITCH
    bm = MOE_BLOCK
    rows = pl.BlockSpec((bm * ROW_PITCH, LANES), lambda i, be, nb: (i, 0))
    wspec = pl.BlockSpec((None, None, D_EXPERT, D_MODEL), lambda i, be, nb: (layer, be[i], 0, 0))
    return pl.pallas_call(
        _moe_kernel,
        grid_spec=pltpu.PrefetchScalarGridSpec(
            num_scalar_prefetch=2,
            grid=(n_rows // bm,),
            in_specs=[rows, wspec, wspec, wspec],
            out_specs=rows,
        ),
        out_shape=jax.ShapeDtypeStruct((n_rows * ROW_PITCH, LANES), jnp.uint32),
        compiler_params=_cparams(("arbitrary",)),
    )(block_e, n_used, xs, wgt_all, wut_all, wd_all)


def _combine_kernel(dest_ref, x_ref, w_ref, yb_ref, g_ref, b_ref, o_ref, ob_ref, buf_ref, sem, *, tm, n, alpha):
    i = pl.program_id(0)
    half = i % 2

    def gather(step, into):
        base = step * tm

        def start(r, carry):
            for k in range(2):
                pltpu.make_async_copy(yb_ref.at[pl.ds(dest_ref[k * n + base + r] * ROW_PITCH, ROW_CHUNKS)],
                                      buf_ref.at[into, k, pl.ds(r * ROW_PITCH, ROW_CHUNKS)],
                                      sem.at[into]).start(priority=k)
            return carry

        lax.fori_loop(0, tm, start, 0, unroll=DMA_UNROLL)

    @pl.when(i == 0)
    def _():
        gather(0, 0)

    @pl.when(i + 1 < pl.num_programs(0))
    def _():
        gather(i + 1, 1 - half)

    for k in range(2):
        pltpu.make_async_copy(yb_ref.at[pl.ds(0, tm * ROW_CHUNKS)], buf_ref.at[half, k, pl.ds(0, tm * ROW_CHUNKS)],
                              sem.at[half]).wait()
    w = w_ref[...]
    y = w[:, 0:1] * _from_rows(buf_ref.at[half, 0], tm) + w[:, 1:2] * _from_rows(buf_ref.at[half, 1], tm)
    x2 = _layer_norm(alpha * x_ref[...] + y, g_ref[...], b_ref[...])
    o_ref[...] = x2
    ob_ref[...] = x2.astype(BF16)


def _combine_ln(x1, w_col, yb, dest, g_all, b_all, layer, alpha):
    n = x1.shape[0]
    tm = _tile(n, COMBINE_TILE)
    row = pl.BlockSpec((tm, D_MODEL), lambda i, d: (i, 0))
    vec = pl.BlockSpec((None, 1, D_MODEL), lambda i, d: (layer, 0, 0))
    return pl.pallas_call(
        functools.partial(_combine_kernel, tm=tm, n=n, alpha=alpha),
        grid_spec=pltpu.PrefetchScalarGridSpec(
            num_scalar_prefetch=1,
            grid=(n // tm,),
            in_specs=[row, pl.BlockSpec((tm, 2), lambda i, d: (i, 0)), pl.BlockSpec(memory_space=pl.ANY), vec, vec],
            out_specs=[row, row],
            scratch_shapes=[pltpu.VMEM((2, 2, tm * ROW_PITCH, LANES), jnp.uint32), pltpu.SemaphoreType.DMA((2,))],
        ),
        out_shape=[jax.ShapeDtypeStruct((n, D_MODEL), F32), jax.ShapeDtypeStruct((n, D_MODEL), BF16)],
        compiler_params=_cparams(("arbitrary",)),
    )(dest, x1, w_col, yb, g_all, b_all)


def _relayout_w_in(w_in):
    wt = jnp.swapaxes(w_in, 1, 2)

    def seg(name):
        a, b = _ORIG[name]
        return wt[:, a:b, :]

    def partner(name):
        a0 = _ORIG[name][0]
        half = RET_DK // 2
        parts = []
        for h in range(N_HEADS):
            b = a0 + RET_DK * h
            parts += [-wt[:, b + half:b + RET_DK, :], wt[:, b:b + half, :]]
        return parts

    pad = jnp.zeros((wt.shape[0], LANES - GLA_RANK + _H_PAD, wt.shape[2]), wt.dtype)
    rows = ([seg(nm) for nm in _SEG512] + [seg("gq"), seg("gk"), seg("rq"), seg("rk")]
            + partner("rq") + partner("rk") + [seg("glr"), pad])
    return jnp.concatenate(rows, axis=1).astype(BF16)


def kernel(x, positions, w_in, w_gla_gate2, b_gla_gate2, gla_norm, hgrn_lb, hgrn_norm, conv_w, conv_b,
           w_lru_a, b_lru_a, w_lru_x, b_lru_x, lru_lambda, w_branch_up, w_merge_gate, b_merge_gate, w_out,
           ln1_g, ln1_b, w_router, router_bias, w_exp_gate, w_exp_up, w_exp_down, ln2_g, ln2_b):
    bsz, seq, d = x.shape
    depth = w_in.shape[0]
    n = bsz * seq
    alpha = (2 * depth) ** 0.25

    w_in_b = _relayout_w_in(w_in)
    w2_b = jnp.pad(w_gla_gate2, ((0, 0), (0, LANES - GLA_RANK), (0, 0))).astype(BF16)
    b2 = b_gla_gate2.reshape(depth, 1, -1)
    gla_nw = jnp.tile(gla_norm, (1, N_HEADS)).reshape(depth, 1, -1)
    hgrn_nw = jnp.tile(hgrn_norm, (1, N_HEADS)).reshape(depth, 1, -1)
    lb_cum = jnp.cumsum(jax.nn.softmax(hgrn_lb.astype(F32), axis=0), axis=0)
    lower = lb_cum - lb_cum[:1]
    log_lb = jnp.log(lower).reshape(depth, 1, -1)
    log_1m_lb = jnp.log1p(-lower).reshape(depth, 1, -1)
    one_m_lb = (1.0 - lower).reshape(depth, 1, -1)
    wa_b = w_lru_a.astype(BF16)
    wx_b = w_lru_x.astype(BF16)
    wg_b = w_merge_gate.astype(BF16)
    wu_b = w_branch_up.astype(BF16)
    bmg = b_merge_gate.reshape(depth, 4, 1, d)
    w_out_b = w_out.astype(BF16)
    wr_t = w_router.T.astype(BF16)
    bias_col = router_bias.reshape(N_EXPERTS, 1).astype(F32)
    ln1g = ln1_g.reshape(depth, 1, d)
    ln1b = ln1_b.reshape(depth, 1, d)
    ln2g = ln2_g.reshape(depth, 1, d)
    ln2b = ln2_b.reshape(depth, 1, d)
    wgt = jnp.swapaxes(w_exp_gate, 2, 3)
    wut = jnp.swapaxes(w_exp_up, 2, 3)

    half = RET_DK // 2
    inv_freq = ROPE_BASE ** (-jnp.arange(half, dtype=F32) / half)
    invf = jnp.tile(inv_freq, 2 * N_HEADS).reshape(1, N_HEADS * RET_DK)
    cos_t, sin_t = _rope_tables(positions.reshape(n, 1), invf)
    cos_t = cos_t.reshape(bsz, seq, -1)
    sin_t = sin_t.reshape(bsz, seq, -1)

    cum, lv = _gated_consts(_tile(seq, GATED_BLOCK))
    ret_c = _tile(seq, RET_BLOCK)
    ret_consts = _ret_consts(ret_c)

    x2d = x.reshape(n, d)
    xb = x2d.astype(BF16)
    for l in range(depth):
        hcat = _in_proj(xb, w_in_b, l).reshape(bsz, seq, H_COLS)
        o_gla = _gla_branch(hcat, w2_b[l], b2[l], gla_nw[l], cum, lv)
        o_hgrn = _hgrn_branch(hcat, log_lb[l], log_1m_lb[l], one_m_lb[l], hgrn_nw[l], cum, lv)
        o_ret = _ret_branch(hcat, cos_t, sin_t, ret_consts, ret_c)
        o_lru = _lru_branch(hcat, conv_w[l], conv_b[l].reshape(1, -1), wa_b[l], b_lru_a[l].reshape(1, -1),
                            wx_b[l], b_lru_x[l].reshape(1, -1), lru_lambda[l].reshape(1, -1))
        outs = tuple(o.reshape(n, BRANCH_WIDTH) for o in (o_gla, o_hgrn, o_ret, o_lru))
        merged = _merge(xb, outs, wg_b, wu_b, bmg, l)
        x1, x1_rows, logits_t = _out_proj_ln(merged, w_out_b, x2d, ln1g, ln1b, wr_t, l, alpha)
        idx_t, wt_t = _route(logits_t, bias_col)
        dest, block_e, n_used, n_rows = _dispatch_plan(idx_t, n)
        xs = _scatter_rows(x1_rows, dest, n_rows)
        yb = _moe_ffn(xs, block_e, n_used, wgt, wut, w_exp_down, l)
        x2d, xb = _combine_ln(x1, wt_t.T, yb, dest, ln2g, ln2b, l, alpha)
    return x2d.reshape(bsz, seq, d)
```

```python
import functools
import math

import numpy as np
import jax
import jax.numpy as jnp
from jax import lax
from jax.experimental import pallas as pl
from jax.experimental.pallas import tpu as pltpu

F32 = jnp.float32
BF16 = jnp.bfloat16
FP8 = jnp.float8_e4m3fn
FP8_TARGET = 240.0

D_MODEL = 2048
N_HEADS = 4
DV = 128
BRANCH_WIDTH = 512
GLA_DK = 64
GLA_RANK = 16
GLA_GATE_NORM = 16.0
HGRN_DK = 128
RET_DK = 64
ROPE_BASE = 10000.0
CONV_WIDTH = 4
LRU_C = 8.0
N_EXPERTS = 32
N_GROUPS = 4
EXPERTS_PER_GROUP = 8
D_EXPERT = 704
NORM_EPS = 1e-5

LANES = 128
SUBLANES = 8
ROW_CHUNKS = D_MODEL // (2 * LANES)
ROW_PITCH = ROW_CHUNKS + 1
VMEM_LIMIT = 56 * 1024 * 1024

_ORIG_SIZES = (("gq", 256), ("gk", 256), ("gv", 512), ("gg", 512), ("glr", 16),
               ("hq", 512), ("hf", 512), ("hi", 512), ("hg", 512),
               ("rq", 256), ("rk", 256), ("rv", 512), ("rg", 512), ("lx", 512), ("ly", 512))
_ORIG = {}
_o = 0
for _n, _w in _ORIG_SIZES:
    _ORIG[_n] = (_o, _o + _w)
    _o += _w

_SEG512 = ("gv", "gg", "hq", "hf", "hi", "hg", "rv", "rg", "lx", "ly")
_SEG256 = ("gq", "gk", "rq", "rk", "rqp", "rkp")
_OFF = {}
_o = 0
for _n in _SEG512:
    _OFF[_n] = _o
    _o += 512
for _n in _SEG256:
    _OFF[_n] = _o
    _o += 256
_OFF["glr"] = _o
_o += LANES
H_COLS = 6912
_H_PAD = H_COLS - _o

GATED_BLOCK = 128
RET_BLOCK = 256
LRU_BLOCK = 256
SEQ_GROUP = 4
DMA_UNROLL = 8
MOE_BLOCK = 256
SCATTER_TILE = 512
COMBINE_TILE = 256
RANK_TILE = 1024


def _tile(n, pref):
    return pref if n % pref == 0 else n


def _cparams(sem):
    return pltpu.CompilerParams(dimension_semantics=sem, vmem_limit_bytes=VMEM_LIMIT)


def _dot(a, b):
    return jnp.dot(a, b, preferred_element_type=F32)


def _dot_nt(a, b):
    return lax.dot_general(a, b, (((1,), (1,)), ((), ())), preferred_element_type=F32)


def _dot_tn(a, b):
    return lax.dot_general(a, b, (((0,), (0,)), ((), ())), preferred_element_type=F32)


def _sigmoid(x):
    return 1.0 / (1.0 + jnp.exp(-x))


def _log_sigmoid(x):
    return jnp.minimum(x, 0.0) - jnp.log(1.0 + jnp.exp(-jnp.abs(x)))


def _softplus(x):
    return jnp.maximum(x, 0.0) + jnp.log(1.0 + jnp.exp(-jnp.abs(x)))


def _layer_norm(y, g, b):
    mu = jnp.mean(y, axis=-1, keepdims=True)
    yc = y - mu
    var = jnp.mean(yc * yc, axis=-1, keepdims=True)
    return yc * lax.rsqrt(var + NORM_EPS) * g + b


def _mm_kernel(x_ref, wt_ref, o_ref):
    o_ref[...] = _dot_nt(x_ref[...], wt_ref[...]).astype(o_ref.dtype)


def _in_proj(xb, wt_all, layer):
    n, k = xb.shape
    cols = wt_all.shape[1]
    tm = _tile(n, 512)
    tn = 2304
    return pl.pallas_call(
        _mm_kernel,
        grid=(cols // tn, n // tm),
        in_specs=[pl.BlockSpec((tm, k), lambda j, i: (i, 0)),
                  pl.BlockSpec((None, tn, k), lambda j, i: (layer, j, 0))],
        out_specs=pl.BlockSpec((tm, tn), lambda j, i: (i, j)),
        out_shape=jax.ShapeDtypeStruct((n, cols), F32),
        compiler_params=_cparams(("arbitrary", "arbitrary")),
    )(xb, wt_all)


def _gated_consts(c):
    n16 = c // 16
    i = np.arange(c)
    g = i // 16
    hs = []
    h = 1
    while h <= n16:
        hs.append(h)
        h *= 2
    same16 = g[:, None] == g[None, :]
    mats = [same16 & (i[None, :] <= i[:, None]), same16 & (i[None, :] > i[:, None])]
    gq = np.arange(n16)[:, None]
    gt = g[None, :]
    for h in hs[1:]:
        lo = (gq // h) * h
        mats.append((gt >= lo) & (gt < gq))
        mats.append((gt > gq) & (gt <= lo + h - 1))
    cum = np.concatenate(mats, 0).astype(np.float32)
    lv = -np.ones((c, c), np.int32)
    lv[(g[:, None] == g[None, :]) & (i[None, :] <= i[:, None])] = 0
    gi = g[:, None]
    gj = g[None, :]
    for li, h in enumerate(hs[:-1]):
        m = (gi // (2 * h) == gj // (2 * h)) & ((gi // h) % 2 == 1) & ((gj // h) % 2 == 0)
        lv[m] = li + 1
    return jnp.asarray(cum, BF16), jnp.asarray(lv)


def _exact_select_dot(sel, x):
    hi = x.astype(BF16)
    r1 = x - hi.astype(F32)
    mid = r1.astype(BF16)
    lo = (r1 - mid.astype(F32)).astype(BF16)
    return _dot(sel, hi) + _dot(sel, mid) + _dot(sel, lo)


def _expand_groups(t):
    w = t.shape[1]
    return jnp.concatenate([jnp.broadcast_to(t[g:g + 1, :], (16, w)) for g in range(t.shape[0])], axis=0)


def _gated_core(q, k, la, v_ref, g_ref, nw_ref, cum_ref, lv_ref, states, *, c, dk):
    ex = _exact_select_dot(cum_ref[...], la)
    n16 = c // 16
    nl = 1 + (ex.shape[0] - 2 * c) // (2 * n16)
    c_t = [ex[0:c]]
    c_e = [ex[c:2 * c]]
    for l in range(1, nl):
        r0 = 2 * c + 2 * (l - 1) * n16
        c_t.append(c_t[0] + _expand_groups(ex[r0:r0 + n16]))
        c_e.append(c_e[0] + _expand_groups(ex[r0 + n16:r0 + 2 * n16]))
    qd = [(q * jnp.exp(c_t[l])).astype(BF16) for l in range(nl)]
    k_diag = (k * jnp.exp(-c_t[0])).astype(BF16)
    kd = [(k * jnp.exp(c_e[l])).astype(BF16) for l in range(nl)]
    dec = jnp.exp(c_t[nl - 1][c - 1:c, :])
    lv = lv_ref[...]
    outs = []
    new_states = []
    for h in range(N_HEADS):
        sk = slice(h * dk, (h + 1) * dk)
        sv = slice(h * DV, (h + 1) * DV)
        p = jnp.where(lv == 0, _dot_nt(qd[0][:, sk], k_diag[:, sk]), 0.0)
        for l in range(nl - 1):
            p = p + jnp.where(lv == l + 1, _dot_nt(qd[l][:, sk], kd[l][:, sk]), 0.0)
        vh = v_ref[:, sv].astype(BF16)
        st = states[h]
        o = _dot(p.astype(BF16), vh) + _dot_nt(qd[nl - 1][:, sk], st.astype(BF16))
        new_states.append(st * dec[:, sk] + _dot_tn(vh, kd[nl - 1][:, sk]))
        ms = jnp.mean(o * o, axis=-1, keepdims=True)
        gh = g_ref[:, sv]
        outs.append((o * lax.rsqrt(ms + NORM_EPS) * nw_ref[:, sv] * (gh * _sigmoid(gh))).astype(BF16))
    return outs, new_states


def _load_states(st_ref):
    return [[st_ref[bi, h] for h in range(N_HEADS)] for bi in range(st_ref.shape[0])]


def _store_results(results, o_ref, st_ref):
    for bi, (outs, new_states) in enumerate(results):
        for h in range(N_HEADS):
            o_ref[bi, :, h * DV:(h + 1) * DV] = outs[h]
            st_ref[bi, h] = new_states[h]


def _gla_kernel(q_ref, k_ref, v_ref, g_ref, lr_ref, w2_ref, b2_ref, nw_ref, cum_ref, lv_ref,
                o_ref, st_ref, *, c):
    @pl.when(pl.program_id(1) == 0)
    def _():
        st_ref[...] = jnp.zeros_like(st_ref)

    states = _load_states(st_ref)
    results = []
    for bi in range(q_ref.shape[0]):
        q = q_ref[bi] * (GLA_DK ** -0.5)
        logits = _dot(lr_ref[bi].astype(BF16), w2_ref[...]) + b2_ref[...]
        la = _log_sigmoid(logits) * (1.0 / GLA_GATE_NORM)
        results.append(_gated_core(q, k_ref[bi], la, v_ref.at[bi], g_ref.at[bi], nw_ref, cum_ref, lv_ref,
                                   states[bi], c=c, dk=GLA_DK))
    _store_results(results, o_ref, st_ref)


def _hgrn_kernel(q_ref, f_ref, v_ref, g_ref, llb_ref, l1m_ref, oml_ref, nw_ref, cum_ref, lv_ref,
                 o_ref, st_ref, *, c):
    @pl.when(pl.program_id(1) == 0)
    def _():
        st_ref[...] = jnp.zeros_like(st_ref)

    states = _load_states(st_ref)
    results = []
    for bi in range(q_ref.shape[0]):
        z = f_ref[bi]
        a = llb_ref[...]
        b = l1m_ref[...] + _log_sigmoid(z)
        la = jnp.maximum(a, b) + jnp.log(1.0 + jnp.exp(-jnp.abs(a - b)))
        k = oml_ref[...] * _sigmoid(-z)
        qq = q_ref[bi]
        q = qq * _sigmoid(qq)
        results.append(_gated_core(q, k, la, v_ref.at[bi], g_ref.at[bi], nw_ref, cum_ref, lv_ref,
                                   states[bi], c=c, dk=HGRN_DK))
    _store_results(results, o_ref, st_ref)


def _seq_group(bsz):
    for g in (SEQ_GROUP, 2):
        if bsz % g == 0:
            return g
    return 1


def _col_spec(g, c, width, name):
    cb = _OFF[name] // width
    return pl.BlockSpec((g, c, width), lambda b, j: (b, j, cb))


def _seq_out(g, c, bsz, seq):
    return (pl.BlockSpec((g, c, BRANCH_WIDTH), lambda b, j: (b, j, 0)),
            jax.ShapeDtypeStruct((bsz, seq, BRANCH_WIDTH), BF16))


def _row_spec(width):
    return pl.BlockSpec((1, width), lambda b, j: (0, 0))


def _const_spec(shape):
    nd = len(shape)
    return pl.BlockSpec(shape, lambda b, j: (0,) * nd)


def _gla_branch(hcat, w2, b2, nw, cum, lv):
    bsz, seq, _ = hcat.shape
    c = _tile(seq, GATED_BLOCK)
    g = _seq_group(bsz)
    out_spec, out_shape = _seq_out(g, c, bsz, seq)
    return pl.pallas_call(
        functools.partial(_gla_kernel, c=c),
        grid=(bsz // g, seq // c),
        in_specs=[_col_spec(g, c, 256, "gq"), _col_spec(g, c, 256, "gk"),
                  _col_spec(g, c, 512, "gv"), _col_spec(g, c, 512, "gg"),
                  _col_spec(g, c, LANES, "glr"),
                  _const_spec(w2.shape), _row_spec(256), _row_spec(512),
                  _const_spec(cum.shape), _const_spec(lv.shape)],
        out_specs=out_spec,
        out_shape=out_shape,
        scratch_shapes=[pltpu.VMEM((g, N_HEADS, DV, GLA_DK), F32)],
        compiler_params=_cparams(("arbitrary", "arbitrary")),
    )(hcat, hcat, hcat, hcat, hcat, w2, b2, nw, cum, lv)


def _hgrn_branch(hcat, llb, l1m, oml, nw, cum, lv):
    bsz, seq, _ = hcat.shape
    c = _tile(seq, GATED_BLOCK)
    g = _seq_group(bsz)
    out_spec, out_shape = _seq_out(g, c, bsz, seq)
    return pl.pallas_call(
        functools.partial(_hgrn_kernel, c=c),
        grid=(bsz // g, seq // c),
        in_specs=[_col_spec(g, c, 512, "hq"), _col_spec(g, c, 512, "hf"),
                  _col_spec(g, c, 512, "hi"), _col_spec(g, c, 512, "hg"),
                  _row_spec(512), _row_spec(512), _row_spec(512), _row_spec(512),
                  _const_spec(cum.shape), _const_spec(lv.shape)],
        out_specs=out_spec,
        out_shape=out_shape,
        scratch_shapes=[pltpu.VMEM((g, N_HEADS, DV, HGRN_DK), F32)],
        compiler_params=_cparams(("arbitrary", "arbitrary")),
    )(hcat, hcat, hcat, hcat, llb, l1m, oml, nw, cum, lv)


def _rope_kernel(pos_ref, invf_ref, cos_ref, sin_ref):
    ang = pos_ref[...].astype(F32) * invf_ref[...]
    cos_ref[...] = jnp.cos(ang)
    sin_ref[...] = jnp.sin(ang)


def _rope_tables(pos_col, invf):
    n = pos_col.shape[0]
    tm = _tile(n, 2048)
    w = invf.shape[1]
    return pl.pallas_call(
        _rope_kernel,
        grid=(n // tm,),
        in_specs=[pl.BlockSpec((tm, 1), lambda i: (i, 0)), pl.BlockSpec((1, w), lambda i: (0, 0))],
        out_specs=[pl.BlockSpec((tm, w), lambda i: (i, 0))] * 2,
        out_shape=[jax.ShapeDtypeStruct((n, w), F32)] * 2,
        compiler_params=_cparams(("arbitrary",)),
    )(pos_col, invf)


def _ret_consts(c):
    hh = np.arange(N_HEADS, dtype=np.float64)
    log_gamma = np.log(1.0 - 2.0 ** (-5.0 - hh))
    pos = np.arange(c, dtype=np.float64)
    diff = pos[:, None] - pos[None, :]
    dm = np.where(diff >= 0, np.exp(np.where(diff >= 0, diff, 0.0)[None] * log_gamma[:, None, None]), 0.0)
    col_head = np.arange(N_HEADS * RET_DK) // RET_DK
    qs = np.exp((pos + 1.0)[:, None] * log_gamma[col_head][None, :])
    ks = np.exp((c - 1.0 - pos)[:, None] * log_gamma[col_head][None, :])
    gc = np.exp(c * log_gamma[col_head])[None, :]
    return (jnp.asarray(dm, F32), jnp.asarray(qs, F32), jnp.asarray(ks, F32), jnp.asarray(gc, F32))


def _ret_kernel(q_ref, k_ref, qp_ref, kp_ref, v_ref, g_ref, cos_ref, sin_ref, dm_ref, qs_ref, ks_ref,
                gc_ref, o_ref, st_ref):
    @pl.when(pl.program_id(1) == 0)
    def _():
        st_ref[...] = jnp.zeros_like(st_ref)

    gc = gc_ref[...]
    states = _load_states(st_ref)
    results = []
    for bi in range(q_ref.shape[0]):
        outs = []
        new_states = []
        cs = cos_ref[bi]
        sn = sin_ref[bi]
        q = q_ref[bi] * cs + qp_ref[bi] * sn
        k = (k_ref[bi] * cs + kp_ref[bi] * sn) * (RET_DK ** -0.5)
        qb = q.astype(BF16)
        kb = k.astype(BF16)
        q_dec = (q * qs_ref[...]).astype(BF16)
        k_dec = (k * ks_ref[...]).astype(BF16)
        for h in range(N_HEADS):
            sk = slice(h * RET_DK, (h + 1) * RET_DK)
            sv = slice(h * DV, (h + 1) * DV)
            s = _dot_nt(qb[:, sk], kb[:, sk]) * dm_ref[h]
            vh = v_ref[bi, :, sv].astype(BF16)
            st = states[bi][h]
            o = _dot(s.astype(BF16), vh) + _dot_nt(q_dec[:, sk], st.astype(BF16))
            new_states.append(st * gc[:, sk] + _dot_tn(vh, k_dec[:, sk]))
            ms = jnp.mean(o * o, axis=-1, keepdims=True)
            gh = g_ref[bi, :, sv]
            outs.append((o * lax.rsqrt(ms + NORM_EPS) * (gh * _sigmoid(gh))).astype(BF16))
        results.append((outs, new_states))
    _store_results(results, o_ref, st_ref)


def _ret_branch(hcat, cos_t, sin_t, consts, c):
    bsz, seq, _ = hcat.shape
    g = _seq_group(bsz)
    out_spec, out_shape = _seq_out(g, c, bsz, seq)
    dm, qs, ks, gc = consts
    tab = pl.BlockSpec((g, c, 256), lambda b, j: (b, j, 0))
    return pl.pallas_call(
        _ret_kernel,
        grid=(bsz // g, seq // c),
        in_specs=[_col_spec(g, c, 256, "rq"), _col_spec(g, c, 256, "rk"),
                  _col_spec(g, c, 256, "rqp"), _col_spec(g, c, 256, "rkp"),
                  _col_spec(g, c, 512, "rv"), _col_spec(g, c, 512, "rg"), tab, tab,
                  _const_spec(dm.shape), _const_spec(qs.shape), _const_spec(ks.shape), _row_spec(256)],
        out_specs=out_spec,
        out_shape=out_shape,
        scratch_shapes=[pltpu.VMEM((g, N_HEADS, DV, RET_DK), F32)],
        compiler_params=_cparams(("arbitrary", "arbitrary")),
    )(hcat, hcat, hcat, hcat, hcat, hcat, cos_t, sin_t, dm, qs, ks, gc)


def _lru_kernel(x_ref, y_ref, cw_ref, cb_ref, wa_ref, ba_ref, wx_ref, bx_ref, lam_ref, o_ref,
                halo_ref, h_ref, *, c):
    @pl.when(pl.program_id(1) == 0)
    def _():
        halo_ref[...] = jnp.zeros_like(halo_ref)
        h_ref[...] = jnp.zeros_like(h_ref)

    cw = cw_ref[...]
    sp = _softplus(-lam_ref[...])
    rows = lax.broadcasted_iota(jnp.int32, (c, 1), 0)
    blk = BRANCH_WIDTH // 4
    for bi in range(x_ref.shape[0]):
        x = x_ref[bi]
        x_ext = jnp.concatenate([halo_ref[bi], x], axis=0)
        xc = cw[CONV_WIDTH - 1:CONV_WIDTH, :] * x
        for w in range(CONV_WIDTH - 1):
            shift = CONV_WIDTH - 1 - w
            xc = xc + cw[w:w + 1, :] * pltpu.roll(x_ext, shift, 0)[SUBLANES:, :]
        xc = xc + cb_ref[...]
        halo_ref[bi] = x[c - SUBLANES:, :]

        ra = []
        rx = []
        for g in range(4):
            xg = xc[:, g * blk:(g + 1) * blk].astype(BF16)
            ra.append(_dot(xg, wa_ref[g]))
            rx.append(_dot(xg, wx_ref[g]))
        r = _sigmoid(jnp.concatenate(ra, axis=1) + ba_ref[...])
        ig = _sigmoid(jnp.concatenate(rx, axis=1) + bx_ref[...])
        log_a = -LRU_C * r * sp
        a = jnp.exp(log_a)
        u = jnp.sqrt(-jnp.tanh(log_a) * (a * a + 1.0)) * (ig * xc)

        s = 1
        while s < c:
            keep = rows >= s
            a_sh = jnp.where(keep, pltpu.roll(a, s, 0), 1.0)
            u_sh = jnp.where(keep, pltpu.roll(u, s, 0), 0.0)
            u = a * u_sh + u
            a = a * a_sh
            s *= 2
        hs = u + a * h_ref[bi, 0:1, :]
        h_ref[bi] = jnp.broadcast_to(hs[c - 1:c, :], (SUBLANES, BRANCH_WIDTH))
        y = y_ref[bi]
        gelu = 0.5 * y * (1.0 + jnp.tanh(math.sqrt(2.0 / math.pi) * (y + 0.044715 * (y * y * y))))
        o_ref[bi] = (hs * gelu).astype(o_ref.dtype)


def _lru_branch(hcat, cw, cb, wa, ba, wx, bx, lam):
    bsz, seq, _ = hcat.shape
    c = _tile(seq, LRU_BLOCK)
    g = _seq_group(bsz)
    out_spec, out_shape = _seq_out(g, c, bsz, seq)
    return pl.pallas_call(
        functools.partial(_lru_kernel, c=c),
        grid=(bsz // g, seq // c),
        in_specs=[_col_spec(g, c, 512, "lx"), _col_spec(g, c, 512, "ly"),
                  _const_spec(cw.shape), _row_spec(512), _const_spec(wa.shape), _row_spec(512),
                  _const_spec(wx.shape), _row_spec(512), _row_spec(512)],
        out_specs=out_spec,
        out_shape=out_shape,
        scratch_shapes=[pltpu.VMEM((g, SUBLANES, BRANCH_WIDTH), F32), pltpu.VMEM((g, SUBLANES, BRANCH_WIDTH), F32)],
        compiler_params=_cparams(("arbitrary", "arbitrary")),
    )(hcat, hcat, cw, cb, wa, ba, wx, bx, lam)


def _pow2_scale(bound):
    return jnp.exp2(jnp.floor(jnp.log2(FP8_TARGET / jnp.maximum(bound, 1e-30))))


def _merge_kernel(x_ref, sx_ref, o0_ref, o1_ref, o2_ref, o3_ref, wg_ref, wu_ref, sc_ref, b_ref, out_ref):
    x = (x_ref[...] * sx_ref[...].astype(BF16)).astype(FP8)
    acc = None
    for br, o_ref in enumerate((o0_ref, o1_ref, o2_ref, o3_ref)):
        gate = _sigmoid(_dot(x, wg_ref[br]) * sc_ref[br] + b_ref[br])
        term = gate * _dot(o_ref[...], wu_ref[br])
        acc = term if acc is None else acc + term
    out_ref[...] = acc.astype(out_ref.dtype)


def _merge(xb, sx, outs, wg_all, wu_all, sc, b_all, layer):
    n = xb.shape[0]
    tm = _tile(n, 1024)
    tn = 512
    ospec = pl.BlockSpec((tm, BRANCH_WIDTH), lambda i, j: (i, 0))
    return pl.pallas_call(
        _merge_kernel,
        grid=(n // tm, D_MODEL // tn),
        in_specs=[pl.BlockSpec((tm, D_MODEL), lambda i, j: (i, 0)), pl.BlockSpec((1, 1), lambda i, j: (0, 0)),
                  ospec, ospec, ospec, ospec,
                  pl.BlockSpec((None, 4, D_MODEL, tn), lambda i, j: (layer, 0, 0, j)),
                  pl.BlockSpec((None, 4, BRANCH_WIDTH, tn), lambda i, j: (layer, 0, 0, j)),
                  pl.BlockSpec((4, 1, tn), lambda i, j: (0, 0, j)),
                  pl.BlockSpec((None, 4, 1, tn), lambda i, j: (layer, 0, 0, j))],
        out_specs=pl.BlockSpec((tm, tn), lambda i, j: (i, j)),
        out_shape=jax.ShapeDtypeStruct((n, D_MODEL), BF16),
        compiler_params=_cparams(("arbitrary", "arbitrary")),
    )(xb, sx, *outs, wg_all, wu_all, sc, b_all)


_HI16 = 0xFFFF0000


def _pack_bf16_pairs(val):
    half = val.shape[1] // 2
    lo = lax.bitcast_convert_type(val[:, :half].astype(BF16).astype(F32), jnp.uint32)
    hi = lax.bitcast_convert_type(val[:, half:].astype(BF16).astype(F32), jnp.uint32)
    return (lo >> 16) | (hi & jnp.uint32(_HI16))


def _unpack_bf16_pairs(words):
    lo = lax.bitcast_convert_type(words << 16, F32)
    hi = lax.bitcast_convert_type(words & jnp.uint32(_HI16), F32)
    return jnp.concatenate([lo, hi], axis=1)


def _to_rows(ref, val):
    m = val.shape[0]
    words = _pack_bf16_pairs(val)
    for j in range(ROW_CHUNKS):
        ref[pl.ds(j, m, stride=ROW_PITCH), :] = words[:, j * LANES:(j + 1) * LANES]
    for j in range(ROW_CHUNKS, ROW_PITCH):
        ref[pl.ds(j, m, stride=ROW_PITCH), :] = jnp.zeros((m, LANES), jnp.uint32)


def _from_rows(ref, m):
    words = jnp.concatenate([ref[pl.ds(j, m, stride=ROW_PITCH), :] for j in range(ROW_CHUNKS)], axis=1)
    return _unpack_bf16_pairs(words)


def _outln_kernel(m_ref, w_ref, x_ref, g_ref, b_ref, wr_ref, x1_ref, x1r_ref, lg_ref, *, alpha):
    y = alpha * x_ref[...] + _dot(m_ref[...], w_ref[...])
    x1 = _layer_norm(y, g_ref[...], b_ref[...])
    x1_ref[...] = x1
    _to_rows(x1r_ref, x1)
    lg_ref[...] = _dot_nt(wr_ref[...], x1.astype(BF16))


def _out_proj_ln(merged, w_all, x, g_all, b_all, wr_t, layer, alpha):
    n = x.shape[0]
    tm = _tile(n, 512)
    row = pl.BlockSpec((tm, D_MODEL), lambda i: (i, 0))
    vec = pl.BlockSpec((None, 1, D_MODEL), lambda i: (layer, 0, 0))
    return pl.pallas_call(
        functools.partial(_outln_kernel, alpha=alpha),
        grid=(n // tm,),
        in_specs=[row, pl.BlockSpec((None, D_MODEL, D_MODEL), lambda i: (layer, 0, 0)), row, vec, vec,
                  pl.BlockSpec((N_EXPERTS, D_MODEL), lambda i: (0, 0))],
        out_specs=[row, pl.BlockSpec((tm * ROW_PITCH, LANES), lambda i: (i, 0)),
                   pl.BlockSpec((N_EXPERTS, tm), lambda i: (0, i))],
        out_shape=[jax.ShapeDtypeStruct((n, D_MODEL), F32), jax.ShapeDtypeStruct((n * ROW_PITCH, LANES), jnp.uint32),
                   jax.ShapeDtypeStruct((N_EXPERTS, n), F32)],
        compiler_params=_cparams(("arbitrary",)),
    )(merged, w_all, x, g_all, b_all, wr_t)


def _route_kernel(lg_ref, bias_ref, idx_ref, wt_ref):
    s = _sigmoid(lg_ref[...])
    sel = s + bias_ref[...]
    tn = s.shape[1]
    rows = lax.broadcasted_iota(jnp.int32, (EXPERTS_PER_GROUP, tn), 0)
    neg = jnp.float32(-jnp.inf)

    def first_max(blk):
        m = jnp.max(blk, axis=0, keepdims=True)
        i = jnp.min(jnp.where(blk == m, rows, EXPERTS_PER_GROUP), axis=0, keepdims=True)
        return m, i

    best = None
    for g in range(N_GROUPS):
        sl = slice(g * EXPERTS_PER_GROUP, (g + 1) * EXPERTS_PER_GROUP)
        blk = sel[sl]
        sg = s[sl]
        m1, i1 = first_max(blk)
        m2, i2 = first_max(jnp.where(rows == i1, neg, blk))
        w1 = jnp.sum(jnp.where(rows == i1, sg, 0.0), axis=0, keepdims=True)
        w2 = jnp.sum(jnp.where(rows == i2, sg, 0.0), axis=0, keepdims=True)
        cand = (m1 + m2, i1 + g * EXPERTS_PER_GROUP, i2 + g * EXPERTS_PER_GROUP, w1, w2)
        if best is None:
            best = cand
        else:
            take = cand[0] > best[0]
            best = tuple(jnp.where(take, cn, bs) for cn, bs in zip(cand, best))
    _, e1, e2, w1, w2 = best
    tot = w1 + w2
    idx_ref[0:1, :] = e1
    idx_ref[1:2, :] = e2
    wt_ref[0:1, :] = w1 / tot
    wt_ref[1:2, :] = w2 / tot


def _route(logits_t, bias_col):
    n = logits_t.shape[1]
    tn = _tile(n, 2048)
    return pl.pallas_call(
        _route_kernel,
        grid=(n // tn,),
        in_specs=[pl.BlockSpec((N_EXPERTS, tn), lambda i: (0, i)),
                  pl.BlockSpec((N_EXPERTS, 1), lambda i: (0, 0))],
        out_specs=[pl.BlockSpec((2, tn), lambda i: (0, i))] * 2,
        out_shape=[jax.ShapeDtypeStruct((2, n), jnp.int32), jax.ShapeDtypeStruct((2, n), F32)],
        compiler_params=_cparams(("arbitrary",)),
    )(logits_t, bias_col)


def _rank_kernel(e_ref, u_ref, rank_ref, cnt_ref, carry_ref):
    @pl.when(pl.program_id(0) == 0)
    def _():
        carry_ref[...] = jnp.zeros_like(carry_ref)

    e = e_ref[...]
    tn = e.shape[1]
    rows = lax.broadcasted_iota(jnp.int32, (N_EXPERTS, tn), 0)
    onehot = jnp.where(rows == e, 1.0, 0.0)
    before = _dot(onehot.astype(BF16), u_ref[...])
    carry = carry_ref[...]
    rank_ref[...] = jnp.sum(onehot * (before + carry), axis=0, keepdims=True).astype(jnp.int32)
    carry = carry + jnp.sum(onehot, axis=1, keepdims=True)
    carry_ref[...] = carry
    cnt_ref[...] = carry


def _rank(e_row):
    m = e_row.shape[1]
    tn = _tile(m, RANK_TILE)
    upper = jnp.asarray(np.triu(np.ones((tn, tn), np.float32), 1), BF16)
    return pl.pallas_call(
        _rank_kernel,
        grid=(m // tn,),
        in_specs=[pl.BlockSpec((1, tn), lambda i: (0, i)), pl.BlockSpec((tn, tn), lambda i: (0, 0))],
        out_specs=[pl.BlockSpec((1, tn), lambda i: (0, i)), pl.BlockSpec((N_EXPERTS, 1), lambda i: (0, 0))],
        out_shape=[jax.ShapeDtypeStruct((1, m), jnp.int32), jax.ShapeDtypeStruct((N_EXPERTS, 1), F32)],
        scratch_shapes=[pltpu.VMEM((N_EXPERTS, 1), F32)],
        compiler_params=_cparams(("arbitrary",)),
    )(e_row, upper)


def _dispatch_plan(idx_t, n):
    m = 2 * n
    bm = MOE_BLOCK
    e_row = idx_t.reshape(1, m)
    rank, cnt = _rank(e_row)
    counts = cnt.reshape(N_EXPERTS).astype(jnp.int32)
    padded = (counts + bm - 1) // bm * bm
    pad_end = jnp.cumsum(padded)
    pad_start = pad_end - padded
    experts = jnp.arange(N_EXPERTS, dtype=jnp.int32)
    start_of = jnp.sum(jnp.where(e_row == experts[:, None], pad_start[:, None], 0), axis=0)
    dest = (rank.reshape(m) + start_of).astype(jnp.int32)
    n_rows = (m + N_EXPERTS * (bm - 1) + bm - 1) // bm * bm
    n_blocks = n_rows // bm
    starts = jnp.arange(n_blocks, dtype=jnp.int32) * bm
    block_e = jnp.minimum(jnp.sum((pad_end[None, :] <= starts[:, None]).astype(jnp.int32), axis=1),
                          N_EXPERTS - 1).astype(jnp.int32)
    n_used = (pad_end[-1] // bm).astype(jnp.int32).reshape(1)
    return dest, block_e, n_used, n_rows


def _scatter_kernel(dest_ref, x_ref, init_ref, out_ref, sem, *, tm, n):
    del init_ref
    base = pl.program_id(0) * tm

    def row_copy(r, k):
        return pltpu.make_async_copy(x_ref.at[pl.ds(r * ROW_PITCH, ROW_CHUNKS)],
                                     out_ref.at[pl.ds(dest_ref[k * n + base + r] * ROW_PITCH, ROW_CHUNKS)], sem)

    def start(r, carry):
        row_copy(r, 0).start(priority=0)
        row_copy(r, 1).start(priority=1)
        return carry

    lax.fori_loop(0, tm, start, 0, unroll=DMA_UNROLL)
    for _ in range(2):
        pltpu.make_async_copy(x_ref.at[pl.ds(0, tm * ROW_CHUNKS)], out_ref.at[pl.ds(0, tm * ROW_CHUNKS)], sem).wait()


def _scatter_rows(x_rows, dest, n_rows):
    n = x_rows.shape[0] // ROW_PITCH
    tm = _tile(n, SCATTER_TILE)
    init = jnp.zeros((n_rows * ROW_PITCH, LANES), x_rows.dtype)
    return pl.pallas_call(
        functools.partial(_scatter_kernel, tm=tm, n=n),
        grid_spec=pltpu.PrefetchScalarGridSpec(
            num_scalar_prefetch=1,
            grid=(n // tm,),
            in_specs=[pl.BlockSpec((tm * ROW_PITCH, LANES), lambda i, d: (i, 0)),
                      pl.BlockSpec(memory_space=pl.ANY)],
            out_specs=pl.BlockSpec(memory_space=pl.ANY),
            scratch_shapes=[pltpu.SemaphoreType.DMA],
        ),
        out_shape=jax.ShapeDtypeStruct((n_rows * ROW_PITCH, LANES), x_rows.dtype),
        input_output_aliases={2: 0},
        compiler_params=_cparams(("arbitrary",)),
    )(dest, x_rows, init)


def _moe_kernel(be_ref, nb_ref, x_ref, wgt_ref, wut_ref, wd_ref, y_ref):
    i = pl.program_id(0)

    @pl.when(i < nb_ref[0])
    def _():
        x = _from_rows(x_ref, MOE_BLOCK).astype(BF16)
        kc = D_MODEL // 4
        g = None
        u = None
        for j in range(4):
            xs = x[:, j * kc:(j + 1) * kc]
            gj = _dot_nt(xs, wgt_ref[:, j * kc:(j + 1) * kc].astype(BF16))
            uj = _dot_nt(xs, wut_ref[:, j * kc:(j + 1) * kc].astype(BF16))
            g = gj if g is None else g + gj
            u = uj if u is None else u + uj
        hid = ((g * _sigmoid(g)) * u).astype(BF16)
        _to_rows(y_ref, _dot(hid, wd_ref[...].astype(BF16)))

    @pl.when(i >= nb_ref[0])
    def _():
        y_ref[...] = jnp.zeros_like(y_ref)


def _moe_ffn(xs, block_e, n_used, wgt_all, wut_all, wd_all, layer):
    n_rows = xs.shape[0] // ROW_PITCH
    bm = MOE_BLOCK
    rows = pl.BlockSpec((bm * ROW_PITCH, LANES), lambda i, be, nb: (i, 0))
    wspec = pl.BlockSpec((None, None, D_EXPERT, D_MODEL), lambda i, be, nb: (layer, be[i], 0, 0))
    return pl.pallas_call(
        _moe_kernel,
        grid_spec=pltpu.PrefetchScalarGridSpec(
            num_scalar_prefetch=2,
            grid=(n_rows // bm,),
            in_specs=[rows, wspec, wspec, wspec],
            out_specs=rows,
        ),
        out_shape=jax.ShapeDtypeStruct((n_rows * ROW_PITCH, LANES), jnp.uint32),
        compiler_params=_cparams(("arbitrary",)),
    )(block_e, n_used, xs, wgt_all, wut_all, wd_all)


def _combine_kernel(dest_ref, x_ref, w_ref, yb_ref, g_ref, b_ref, o_ref, ob_ref, buf_ref, sem, *, tm, n, alpha):
    i = pl.program_id(0)
    half = i % 2

    def gather(step, into):
        base = step * tm

        def start(r, carry):
            for k in range(2):
                pltpu.make_async_copy(yb_ref.at[pl.ds(dest_ref[k * n + base + r] * ROW_PITCH, ROW_CHUNKS)],
                                      buf_ref.at[into, k, pl.ds(r * ROW_PITCH, ROW_CHUNKS)],
                                      sem.at[into]).start(priority=k)
            return carry

        lax.fori_loop(0, tm, start, 0, unroll=DMA_UNROLL)

    @pl.when(i == 0)
    def _():
        gather(0, 0)

    @pl.when(i + 1 < pl.num_programs(0))
    def _():
        gather(i + 1, 1 - half)

    for k in range(2):
        pltpu.make_async_copy(yb_ref.at[pl.ds(0, tm * ROW_CHUNKS)], buf_ref.at[half, k, pl.ds(0, tm * ROW_CHUNKS)],
                              sem.at[half]).wait()
    w = w_ref[...]
    y = w[:, 0:1] * _from_rows(buf_ref.at[half, 0], tm) + w[:, 1:2] * _from_rows(buf_ref.at[half, 1], tm)
    x2 = _layer_norm(alpha * x_ref[...] + y, g_ref[...], b_ref[...])
    o_ref[...] = x2
    ob_ref[...] = x2.astype(BF16)


def _combine_ln(x1, w_col, yb, dest, g_all, b_all, layer, alpha):
    n = x1.shape[0]
    tm = _tile(n, COMBINE_TILE)
    row = pl.BlockSpec((tm, D_MODEL), lambda i, d: (i, 0))
    vec = pl.BlockSpec((None, 1, D_MODEL), lambda i, d: (layer, 0, 0))
    return pl.pallas_call(
        functools.partial(_combine_kernel, tm=tm, n=n, alpha=alpha),
        grid_spec=pltpu.PrefetchScalarGridSpec(
            num_scalar_prefetch=1,
            grid=(n // tm,),
            in_specs=[row, pl.BlockSpec((tm, 2), lambda i, d: (i, 0)), pl.BlockSpec(memory_space=pl.ANY), vec, vec],
            out_specs=[row, row],
            scratch_shapes=[pltpu.VMEM((2, 2, tm * ROW_PITCH, LANES), jnp.uint32), pltpu.SemaphoreType.DMA((2,))],
        ),
        out_shape=[jax.ShapeDtypeStruct((n, D_MODEL), F32), jax.ShapeDtypeStruct((n, D_MODEL), BF16)],
        compiler_params=_cparams(("arbitrary",)),
    )(dest, x1, w_col, yb, g_all, b_all)


def _relayout_w_in(w_in):
    wt = jnp.swapaxes(w_in, 1, 2)

    def seg(name):
        a, b = _ORIG[name]
        return wt[:, a:b, :]

    def partner(name):
        a0 = _ORIG[name][0]
        half = RET_DK // 2
        parts = []
        for h in range(N_HEADS):
            b = a0 + RET_DK * h
            parts += [-wt[:, b + half:b + RET_DK, :], wt[:, b:b + half, :]]
        return parts

    pad = jnp.zeros((wt.shape[0], LANES - GLA_RANK + _H_PAD, wt.shape[2]), wt.dtype)
    rows = ([seg(nm) for nm in _SEG512] + [seg("gq"), seg("gk"), seg("rq"), seg("rk")]
            + partner("rq") + partner("rk") + [seg("glr"), pad])
    return jnp.concatenate(rows, axis=1).astype(BF16)


def kernel(x, positions, w_in, w_gla_gate2, b_gla_gate2, gla_norm, hgrn_lb, hgrn_norm, conv_w, conv_b,
           w_lru_a, b_lru_a, w_lru_x, b_lru_x, lru_lambda, w_branch_up, w_merge_gate, b_merge_gate, w_out,
           ln1_g, ln1_b, w_router, router_bias, w_exp_gate, w_exp_up, w_exp_down, ln2_g, ln2_b):
    bsz, seq, d = x.shape
    depth = w_in.shape[0]
    n = bsz * seq
    alpha = (2 * depth) ** 0.25

    w_in_b = _relayout_w_in(w_in)
    w2_b = jnp.pad(w_gla_gate2, ((0, 0), (0, LANES - GLA_RANK), (0, 0))).astype(BF16)
    b2 = b_gla_gate2.reshape(depth, 1, -1)
    gla_nw = jnp.tile(gla_norm, (1, N_HEADS)).reshape(depth, 1, -1)
    hgrn_nw = jnp.tile(hgrn_norm, (1, N_HEADS)).reshape(depth, 1, -1)
    lb_cum = jnp.cumsum(jax.nn.softmax(hgrn_lb.astype(F32), axis=0), axis=0)
    lower = lb_cum - lb_cum[:1]
    log_lb = jnp.log(lower).reshape(depth, 1, -1)
    log_1m_lb = jnp.log1p(-lower).reshape(depth, 1, -1)
    one_m_lb = (1.0 - lower).reshape(depth, 1, -1)
    wa_b = w_lru_a.astype(BF16)
    wx_b = w_lru_x.astype(BF16)
    sw = _pow2_scale(jnp.max(jnp.abs(w_merge_gate), axis=(2, 3)))
    wg_q = (w_merge_gate * sw[:, :, None, None]).astype(FP8)
    ln_bound = math.sqrt(d) * jnp.max(jnp.abs(ln2_g), axis=1) + jnp.max(jnp.abs(ln2_b), axis=1)
    x_bound = jnp.concatenate([jnp.max(jnp.abs(x)).reshape(1), ln_bound[:depth - 1]])
    sx_all = _pow2_scale(x_bound)
    sc_all = jnp.broadcast_to((1.0 / (sw * sx_all[:, None]))[:, :, None, None], (depth, 4, 1, d)).astype(F32)
    wu_b = w_branch_up.astype(BF16)
    bmg = b_merge_gate.reshape(depth, 4, 1, d)
    w_out_b = w_out.astype(BF16)
    wr_t = w_router.T.astype(BF16)
    bias_col = router_bias.reshape(N_EXPERTS, 1).astype(F32)
    ln1g = ln1_g.reshape(depth, 1, d)
    ln1b = ln1_b.reshape(depth, 1, d)
    ln2g = ln2_g.reshape(depth, 1, d)
    ln2b = ln2_b.reshape(depth, 1, d)
    wgt = jnp.swapaxes(w_exp_gate, 2, 3)
    wut = jnp.swapaxes(w_exp_up, 2, 3)

    half = RET_DK // 2
    inv_freq = ROPE_BASE ** (-jnp.arange(half, dtype=F32) / half)
    invf = jnp.tile(inv_freq, 2 * N_HEADS).reshape(1, N_HEADS * RET_DK)
    cos_t, sin_t = _rope_tables(positions.reshape(n, 1), invf)
    cos_t = cos_t.reshape(bsz, seq, -1)
    sin_t = sin_t.reshape(bsz, seq, -1)

    cum, lv = _gated_consts(_tile(seq, GATED_BLOCK))
    ret_c = _tile(seq, RET_BLOCK)
    ret_consts = _ret_consts(ret_c)

    x2d = x.reshape(n, d)
    xb = x2d.astype(BF16)
    for l in range(depth):
        hcat = _in_proj(xb, w_in_b, l).reshape(bsz, seq, H_COLS)
        o_gla = _gla_branch(hcat, w2_b[l], b2[l], gla_nw[l], cum, lv)
        o_hgrn = _hgrn_branch(hcat, log_lb[l], log_1m_lb[l], one_m_lb[l], hgrn_nw[l], cum, lv)
        o_ret = _ret_branch(hcat, cos_t, sin_t, ret_consts, ret_c)
        o_lru = _lru_branch(hcat, conv_w[l], conv_b[l].reshape(1, -1), wa_b[l], b_lru_a[l].reshape(1, -1),
                            wx_b[l], b_lru_x[l].reshape(1, -1), lru_lambda[l].reshape(1, -1))
        outs = tuple(o.reshape(n, BRANCH_WIDTH) for o in (o_gla, o_hgrn, o_ret, o_lru))
        merged = _merge(xb, sx_all[l].reshape(1, 1), outs, wg_q, wu_b, sc_all[l], bmg, l)
        x1, x1_rows, logits_t = _out_proj_ln(merged, w_out_b, x2d, ln1g, ln1b, wr_t, l, alpha)
        idx_t, wt_t = _route(logits_t, bias_col)
        dest, block_e, n_used, n_rows = _dispatch_plan(idx_t, n)
        xs = _scatter_rows(x1_rows, dest, n_rows)
        yb = _moe_ffn(xs, block_e, n_used, wgt, wut, w_exp_down, l)
        x2d, xb = _combine_ln(x1, wt_t.T, yb, dest, ln2g, ln2b, l, alpha)
    return x2d.reshape(bsz, seq, d)
```

```python
import functools
import math

import numpy as np
import jax
import jax.numpy as jnp
from jax import lax
from jax.experimental import pallas as pl
from jax.experimental.pallas import tpu as pltpu

F32 = jnp.float32
BF16 = jnp.bfloat16
FP8 = jnp.float8_e4m3fn
FP8_TARGET = 240.0

D_MODEL = 2048
N_HEADS = 4
DV = 128
BRANCH_WIDTH = 512
GLA_DK = 64
GLA_RANK = 16
GLA_GATE_NORM = 16.0
HGRN_DK = 128
RET_DK = 64
ROPE_BASE = 10000.0
CONV_WIDTH = 4
LRU_C = 8.0
N_EXPERTS = 32
N_GROUPS = 4
EXPERTS_PER_GROUP = 8
D_EXPERT = 704
NORM_EPS = 1e-5

LANES = 128
SUBLANES = 8
ROW_CHUNKS = D_MODEL // (2 * LANES)
ROW_PITCH = ROW_CHUNKS + 1
VMEM_LIMIT = 56 * 1024 * 1024

_ORIG_SIZES = (("gq", 256), ("gk", 256), ("gv", 512), ("gg", 512), ("glr", 16),
               ("hq", 512), ("hf", 512), ("hi", 512), ("hg", 512),
               ("rq", 256), ("rk", 256), ("rv", 512), ("rg", 512), ("lx", 512), ("ly", 512))
_ORIG = {}
_o = 0
for _n, _w in _ORIG_SIZES:
    _ORIG[_n] = (_o, _o + _w)
    _o += _w

_SEG512 = ("gv", "gg", "hq", "hf", "hi", "hg", "rv", "rg", "lx", "ly")
_SEG256 = ("gq", "gk", "rq", "rk", "rqp", "rkp")
_OFF = {}
_o = 0
for _n in _SEG512:
    _OFF[_n] = _o
    _o += 512
for _n in _SEG256:
    _OFF[_n] = _o
    _o += 256
_OFF["glr"] = _o
_o += LANES
H_COLS = 6912
_H_PAD = H_COLS - _o

GATED_BLOCK = 128
RET_BLOCK = 256
LRU_BLOCK = 256
SEQ_GROUP = 4
DMA_UNROLL = 8
MOE_BLOCK = 256
SCATTER_TILE = 512
COMBINE_TILE = 256
RANK_TILE = 1024


def _tile(n, pref):
    return pref if n % pref == 0 else n


def _cparams(sem):
    return pltpu.CompilerParams(dimension_semantics=sem, vmem_limit_bytes=VMEM_LIMIT)


def _dot(a, b):
    return jnp.dot(a, b, preferred_element_type=F32)


def _dot_nt(a, b):
    return lax.dot_general(a, b, (((1,), (1,)), ((), ())), preferred_element_type=F32)


def _dot_tn(a, b):
    return lax.dot_general(a, b, (((0,), (0,)), ((), ())), preferred_element_type=F32)


def _sigmoid(x):
    return 1.0 / (1.0 + jnp.exp(-x))


def _log_sigmoid(x):
    return jnp.minimum(x, 0.0) - jnp.log(1.0 + jnp.exp(-jnp.abs(x)))


def _softplus(x):
    return jnp.maximum(x, 0.0) + jnp.log(1.0 + jnp.exp(-jnp.abs(x)))


def _layer_norm(y, g, b):
    mu = jnp.mean(y, axis=-1, keepdims=True)
    yc = y - mu
    var = jnp.mean(yc * yc, axis=-1, keepdims=True)
    return yc * lax.rsqrt(var + NORM_EPS) * g + b


def _mm_kernel(x_ref, wt_ref, o_ref):
    o_ref[...] = _dot_nt(x_ref[...], wt_ref[...]).astype(o_ref.dtype)


def _in_proj(xb, wt_all, layer):
    n, k = xb.shape
    cols = wt_all.shape[1]
    tm = _tile(n, 512)
    tn = 2304
    return pl.pallas_call(
        _mm_kernel,
        grid=(cols // tn, n // tm),
        in_specs=[pl.BlockSpec((tm, k), lambda j, i: (i, 0)),
                  pl.BlockSpec((None, tn, k), lambda j, i: (layer, j, 0))],
        out_specs=pl.BlockSpec((tm, tn), lambda j, i: (i, j)),
        out_shape=jax.ShapeDtypeStruct((n, cols), F32),
        compiler_params=_cparams(("arbitrary", "arbitrary")),
    )(xb, wt_all)


def _gated_consts(c):
    n16 = c // 16
    i = np.arange(c)
    g = i // 16
    hs = []
    h = 1
    while h <= n16:
        hs.append(h)
        h *= 2
    same16 = g[:, None] == g[None, :]
    mats = [same16 & (i[None, :] <= i[:, None]), same16 & (i[None, :] > i[:, None])]
    gq = np.arange(n16)[:, None]
    gt = g[None, :]
    for h in hs[1:]:
        lo = (gq // h) * h
        mats.append((gt >= lo) & (gt < gq))
        mats.append((gt > gq) & (gt <= lo + h - 1))
    cum = np.concatenate(mats, 0).astype(np.float32)
    lv = -np.ones((c, c), np.int32)
    lv[(g[:, None] == g[None, :]) & (i[None, :] <= i[:, None])] = 0
    gi = g[:, None]
    gj = g[None, :]
    for li, h in enumerate(hs[:-1]):
        m = (gi // (2 * h) == gj // (2 * h)) & ((gi // h) % 2 == 1) & ((gj // h) % 2 == 0)
        lv[m] = li + 1
    return jnp.asarray(cum, BF16), jnp.asarray(lv)


def _exact_select_dot(sel, x):
    hi = x.astype(BF16)
    r1 = x - hi.astype(F32)
    mid = r1.astype(BF16)
    lo = (r1 - mid.astype(F32)).astype(BF16)
    return _dot(sel, hi) + _dot(sel, mid) + _dot(sel, lo)


def _expand_groups(t):
    w = t.shape[1]
    return jnp.concatenate([jnp.broadcast_to(t[g:g + 1, :], (16, w)) for g in range(t.shape[0])], axis=0)


def _gated_core(q, k, la, v_ref, g_ref, nw_ref, cum_ref, lv_ref, states, *, c, dk):
    ex = _exact_select_dot(cum_ref[...], la)
    n16 = c // 16
    nl = 1 + (ex.shape[0] - 2 * c) // (2 * n16)
    c_t = [ex[0:c]]
    c_e = [ex[c:2 * c]]
    for l in range(1, nl):
        r0 = 2 * c + 2 * (l - 1) * n16
        c_t.append(c_t[0] + _expand_groups(ex[r0:r0 + n16]))
        c_e.append(c_e[0] + _expand_groups(ex[r0 + n16:r0 + 2 * n16]))
    qd = [(q * jnp.exp(c_t[l])).astype(BF16) for l in range(nl)]
    k_diag = (k * jnp.exp(-c_t[0])).astype(BF16)
    kd = [(k * jnp.exp(c_e[l])).astype(BF16) for l in range(nl)]
    dec = jnp.exp(c_t[nl - 1][c - 1:c, :])
    lv = lv_ref[...]
    outs = []
    new_states = []
    for h in range(N_HEADS):
        sk = slice(h * dk, (h + 1) * dk)
        sv = slice(h * DV, (h + 1) * DV)
        p = jnp.where(lv == 0, _dot_nt(qd[0][:, sk], k_diag[:, sk]), 0.0)
        for l in range(nl - 1):
            p = p + jnp.where(lv == l + 1, _dot_nt(qd[l][:, sk], kd[l][:, sk]), 0.0)
        vh = v_ref[:, sv].astype(BF16)
        st = states[h]
        o = _dot(p.astype(BF16), vh) + _dot_nt(qd[nl - 1][:, sk], st.astype(BF16))
        new_states.append(st * dec[:, sk] + _dot_tn(vh, kd[nl - 1][:, sk]))
        ms = jnp.mean(o * o, axis=-1, keepdims=True)
        gh = g_ref[:, sv]
        outs.append((o * lax.rsqrt(ms + NORM_EPS) * nw_ref[:, sv] * (gh * _sigmoid(gh))).astype(BF16))
    return outs, new_states


def _load_states(st_ref):
    return [[st_ref[bi, h] for h in range(N_HEADS)] for bi in range(st_ref.shape[0])]


def _store_results(results, o_ref, st_ref):
    for bi, (outs, new_states) in enumerate(results):
        for h in range(N_HEADS):
            o_ref[bi, :, h * DV:(h + 1) * DV] = outs[h]
            st_ref[bi, h] = new_states[h]


def _gla_kernel(q_ref, k_ref, v_ref, g_ref, lr_ref, w2_ref, b2_ref, nw_ref, cum_ref, lv_ref,
                o_ref, st_ref, *, c):
    @pl.when(pl.program_id(1) == 0)
    def _():
        st_ref[...] = jnp.zeros_like(st_ref)

    states = _load_states(st_ref)
    results = []
    for bi in range(q_ref.shape[0]):
        q = q_ref[bi] * (GLA_DK ** -0.5)
        logits = _dot(lr_ref[bi].astype(BF16), w2_ref[...]) + b2_ref[...]
        la = _log_sigmoid(logits) * (1.0 / GLA_GATE_NORM)
        results.append(_gated_core(q, k_ref[bi], la, v_ref.at[bi], g_ref.at[bi], nw_ref, cum_ref, lv_ref,
                                   states[bi], c=c, dk=GLA_DK))
    _store_results(results, o_ref, st_ref)


def _hgrn_kernel(q_ref, f_ref, v_ref, g_ref, llb_ref, l1m_ref, oml_ref, nw_ref, cum_ref, lv_ref,
                 o_ref, st_ref, *, c):
    @pl.when(pl.program_id(1) == 0)
    def _():
        st_ref[...] = jnp.zeros_like(st_ref)

    states = _load_states(st_ref)
    results = []
    for bi in range(q_ref.shape[0]):
        z = f_ref[bi]
        a = llb_ref[...]
        b = l1m_ref[...] + _log_sigmoid(z)
        la = jnp.maximum(a, b) + jnp.log(1.0 + jnp.exp(-jnp.abs(a - b)))
        k = oml_ref[...] * _sigmoid(-z)
        qq = q_ref[bi]
        q = qq * _sigmoid(qq)
        results.append(_gated_core(q, k, la, v_ref.at[bi], g_ref.at[bi], nw_ref, cum_ref, lv_ref,
                                   states[bi], c=c, dk=HGRN_DK))
    _store_results(results, o_ref, st_ref)


def _seq_group(bsz):
    for g in (SEQ_GROUP, 2):
        if bsz % g == 0:
            return g
    return 1


def _col_spec(g, c, width, name):
    cb = _OFF[name] // width
    return pl.BlockSpec((g, c, width), lambda b, j: (b, j, cb))


def _seq_out(g, c, bsz, seq):
    return (pl.BlockSpec((g, c, BRANCH_WIDTH), lambda b, j: (b, j, 0)),
            jax.ShapeDtypeStruct((bsz, seq, BRANCH_WIDTH), BF16))


def _row_spec(width):
    return pl.BlockSpec((1, width), lambda b, j: (0, 0))


def _const_spec(shape):
    nd = len(shape)
    return pl.BlockSpec(shape, lambda b, j: (0,) * nd)


def _gla_branch(hcat, w2, b2, nw, cum, lv):
    bsz, seq, _ = hcat.shape
    c = _tile(seq, GATED_BLOCK)
    g = _seq_group(bsz)
    out_spec, out_shape = _seq_out(g, c, bsz, seq)
    return pl.pallas_call(
        functools.partial(_gla_kernel, c=c),
        grid=(bsz // g, seq // c),
        in_specs=[_col_spec(g, c, 256, "gq"), _col_spec(g, c, 256, "gk"),
                  _col_spec(g, c, 512, "gv"), _col_spec(g, c, 512, "gg"),
                  _col_spec(g, c, LANES, "glr"),
                  _const_spec(w2.shape), _row_spec(256), _row_spec(512),
                  _const_spec(cum.shape), _const_spec(lv.shape)],
        out_specs=out_spec,
        out_shape=out_shape,
        scratch_shapes=[pltpu.VMEM((g, N_HEADS, DV, GLA_DK), F32)],
        compiler_params=_cparams(("arbitrary", "arbitrary")),
    )(hcat, hcat, hcat, hcat, hcat, w2, b2, nw, cum, lv)


def _hgrn_branch(hcat, llb, l1m, oml, nw, cum, lv):
    bsz, seq, _ = hcat.shape
    c = _tile(seq, GATED_BLOCK)
    g = _seq_group(bsz)
    out_spec, out_shape = _seq_out(g, c, bsz, seq)
    return pl.pallas_call(
        functools.partial(_hgrn_kernel, c=c),
        grid=(bsz // g, seq // c),
        in_specs=[_col_spec(g, c, 512, "hq"), _col_spec(g, c, 512, "hf"),
                  _col_spec(g, c, 512, "hi"), _col_spec(g, c, 512, "hg"),
                  _row_spec(512), _row_spec(512), _row_spec(512), _row_spec(512),
                  _const_spec(cum.shape), _const_spec(lv.shape)],
        out_specs=out_spec,
        out_shape=out_shape,
        scratch_shapes=[pltpu.VMEM((g, N_HEADS, DV, HGRN_DK), F32)],
        compiler_params=_cparams(("arbitrary", "arbitrary")),
    )(hcat, hcat, hcat, hcat, llb, l1m, oml, nw, cum, lv)


def _rope_kernel(pos_ref, invf_ref, cos_ref, sin_ref):
    ang = pos_ref[...].astype(F32) * invf_ref[...]
    cos_ref[...] = jnp.cos(ang)
    sin_ref[...] = jnp.sin(ang)


def _rope_tables(pos_col, invf):
    n = pos_col.shape[0]
    tm = _tile(n, 2048)
    w = invf.shape[1]
    return pl.pallas_call(
        _rope_kernel,
        grid=(n // tm,),
        in_specs=[pl.BlockSpec((tm, 1), lambda i: (i, 0)), pl.BlockSpec((1, w), lambda i: (0, 0))],
        out_specs=[pl.BlockSpec((tm, w), lambda i: (i, 0))] * 2,
        out_shape=[jax.ShapeDtypeStruct((n, w), F32)] * 2,
        compiler_params=_cparams(("arbitrary",)),
    )(pos_col, invf)


def _ret_consts(c):
    hh = np.arange(N_HEADS, dtype=np.float64)
    log_gamma = np.log(1.0 - 2.0 ** (-5.0 - hh))
    pos = np.arange(c, dtype=np.float64)
    diff = pos[:, None] - pos[None, :]
    dm = np.where(diff >= 0, np.exp(np.where(diff >= 0, diff, 0.0)[None] * log_gamma[:, None, None]), 0.0)
    col_head = np.arange(N_HEADS * RET_DK) // RET_DK
    qs = np.exp((pos + 1.0)[:, None] * log_gamma[col_head][None, :])
    ks = np.exp((c - 1.0 - pos)[:, None] * log_gamma[col_head][None, :])
    gc = np.exp(c * log_gamma[col_head])[None, :]
    return (jnp.asarray(dm, F32), jnp.asarray(qs, F32), jnp.asarray(ks, F32), jnp.asarray(gc, F32))


def _ret_kernel(q_ref, k_ref, qp_ref, kp_ref, v_ref, g_ref, cos_ref, sin_ref, dm_ref, qs_ref, ks_ref,
                gc_ref, o_ref, st_ref):
    @pl.when(pl.program_id(1) == 0)
    def _():
        st_ref[...] = jnp.zeros_like(st_ref)

    gc = gc_ref[...]
    states = _load_states(st_ref)
    results = []
    for bi in range(q_ref.shape[0]):
        outs = []
        new_states = []
        cs = cos_ref[bi]
        sn = sin_ref[bi]
        q = q_ref[bi] * cs + qp_ref[bi] * sn
        k = (k_ref[bi] * cs + kp_ref[bi] * sn) * (RET_DK ** -0.5)
        qb = q.astype(BF16)
        kb = k.astype(BF16)
        q_dec = (q * qs_ref[...]).astype(BF16)
        k_dec = (k * ks_ref[...]).astype(BF16)
        for h in range(N_HEADS):
            sk = slice(h * RET_DK, (h + 1) * RET_DK)
            sv = slice(h * DV, (h + 1) * DV)
            s = _dot_nt(qb[:, sk], kb[:, sk]) * dm_ref[h]
            vh = v_ref[bi, :, sv].astype(BF16)
            st = states[bi][h]
            o = _dot(s.astype(BF16), vh) + _dot_nt(q_dec[:, sk], st.astype(BF16))
            new_states.append(st * gc[:, sk] + _dot_tn(vh, k_dec[:, sk]))
            ms = jnp.mean(o * o, axis=-1, keepdims=True)
            gh = g_ref[bi, :, sv]
            outs.append((o * lax.rsqrt(ms + NORM_EPS) * (gh * _sigmoid(gh))).astype(BF16))
        results.append((outs, new_states))
    _store_results(results, o_ref, st_ref)


def _ret_branch(hcat, cos_t, sin_t, consts, c):
    bsz, seq, _ = hcat.shape
    g = _seq_group(bsz)
    out_spec, out_shape = _seq_out(g, c, bsz, seq)
    dm, qs, ks, gc = consts
    tab = pl.BlockSpec((g, c, 256), lambda b, j: (b, j, 0))
    return pl.pallas_call(
        _ret_kernel,
        grid=(bsz // g, seq // c),
        in_specs=[_col_spec(g, c, 256, "rq"), _col_spec(g, c, 256, "rk"),
                  _col_spec(g, c, 256, "rqp"), _col_spec(g, c, 256, "rkp"),
                  _col_spec(g, c, 512, "rv"), _col_spec(g, c, 512, "rg"), tab, tab,
                  _const_spec(dm.shape), _const_spec(qs.shape), _const_spec(ks.shape), _row_spec(256)],
        out_specs=out_spec,
        out_shape=out_shape,
        scratch_shapes=[pltpu.VMEM((g, N_HEADS, DV, RET_DK), F32)],
        compiler_params=_cparams(("arbitrary", "arbitrary")),
    )(hcat, hcat, hcat, hcat, hcat, hcat, cos_t, sin_t, dm, qs, ks, gc)


def _lru_kernel(x_ref, y_ref, cw_ref, cb_ref, wa_ref, ba_ref, wx_ref, bx_ref, lam_ref, o_ref,
                halo_ref, h_ref, *, c):
    @pl.when(pl.program_id(1) == 0)
    def _():
        halo_ref[...] = jnp.zeros_like(halo_ref)
        h_ref[...] = jnp.zeros_like(h_ref)

    cw = cw_ref[...]
    sp = _softplus(-lam_ref[...])
    rows = lax.broadcasted_iota(jnp.int32, (c, 1), 0)
    blk = BRANCH_WIDTH // 4
    for bi in range(x_ref.shape[0]):
        x = x_ref[bi]
        x_ext = jnp.concatenate([halo_ref[bi], x], axis=0)
        xc = cw[CONV_WIDTH - 1:CONV_WIDTH, :] * x
        for w in range(CONV_WIDTH - 1):
            shift = CONV_WIDTH - 1 - w
            xc = xc + cw[w:w + 1, :] * pltpu.roll(x_ext, shift, 0)[SUBLANES:, :]
        xc = xc + cb_ref[...]
        halo_ref[bi] = x[c - SUBLANES:, :]

        ra = []
        rx = []
        for g in range(4):
            xg = xc[:, g * blk:(g + 1) * blk].astype(BF16)
            ra.append(_dot(xg, wa_ref[g]))
            rx.append(_dot(xg, wx_ref[g]))
        r = _sigmoid(jnp.concatenate(ra, axis=1) + ba_ref[...])
        ig = _sigmoid(jnp.concatenate(rx, axis=1) + bx_ref[...])
        log_a = -LRU_C * r * sp
        a = jnp.exp(log_a)
        u = jnp.sqrt(-jnp.tanh(log_a) * (a * a + 1.0)) * (ig * xc)

        s = 1
        while s < c:
            keep = rows >= s
            a_sh = jnp.where(keep, pltpu.roll(a, s, 0), 1.0)
            u_sh = jnp.where(keep, pltpu.roll(u, s, 0), 0.0)
            u = a * u_sh + u
            a = a * a_sh
            s *= 2
        hs = u + a * h_ref[bi, 0:1, :]
        h_ref[bi] = jnp.broadcast_to(hs[c - 1:c, :], (SUBLANES, BRANCH_WIDTH))
        y = y_ref[bi]
        gelu = 0.5 * y * (1.0 + jnp.tanh(math.sqrt(2.0 / math.pi) * (y + 0.044715 * (y * y * y))))
        o_ref[bi] = (hs * gelu).astype(o_ref.dtype)


def _lru_branch(hcat, cw, cb, wa, ba, wx, bx, lam):
    bsz, seq, _ = hcat.shape
    c = _tile(seq, LRU_BLOCK)
    g = _seq_group(bsz)
    out_spec, out_shape = _seq_out(g, c, bsz, seq)
    return pl.pallas_call(
        functools.partial(_lru_kernel, c=c),
        grid=(bsz // g, seq // c),
        in_specs=[_col_spec(g, c, 512, "lx"), _col_spec(g, c, 512, "ly"),
                  _const_spec(cw.shape), _row_spec(512), _const_spec(wa.shape), _row_spec(512),
                  _const_spec(wx.shape), _row_spec(512), _row_spec(512)],
        out_specs=out_spec,
        out_shape=out_shape,
        scratch_shapes=[pltpu.VMEM((g, SUBLANES, BRANCH_WIDTH), F32), pltpu.VMEM((g, SUBLANES, BRANCH_WIDTH), F32)],
        compiler_params=_cparams(("arbitrary", "arbitrary")),
    )(hcat, hcat, cw, cb, wa, ba, wx, bx, lam)


def _pow2_scale(bound):
    return jnp.exp2(jnp.floor(jnp.log2(FP8_TARGET / jnp.maximum(bound, 1e-30))))


def _merge_kernel(x_ref, sx_ref, o0_ref, o1_ref, o2_ref, o3_ref, wg_ref, wu_ref, sc_ref, b_ref, out_ref):
    x = (x_ref[...] * sx_ref[...].astype(BF16)).astype(FP8)
    acc = None
    for br, o_ref in enumerate((o0_ref, o1_ref, o2_ref, o3_ref)):
        gate = _sigmoid(_dot(x, wg_ref[br]) * sc_ref[br] + b_ref[br])
        term = gate * _dot(o_ref[...], wu_ref[br])
        acc = term if acc is None else acc + term
    out_ref[...] = acc.astype(out_ref.dtype)


def _merge(xb, sx, outs, wg_all, wu_all, sc, b_all, layer):
    n = xb.shape[0]
    tm = _tile(n, 1024)
    tn = 512
    ospec = pl.BlockSpec((tm, BRANCH_WIDTH), lambda i, j: (i, 0))
    return pl.pallas_call(
        _merge_kernel,
        grid=(n // tm, D_MODEL // tn),
        in_specs=[pl.BlockSpec((tm, D_MODEL), lambda i, j: (i, 0)), pl.BlockSpec((1, 1), lambda i, j: (0, 0)),
                  ospec, ospec, ospec, ospec,
                  pl.BlockSpec((None, 4, D_MODEL, tn), lambda i, j: (layer, 0, 0, j)),
                  pl.BlockSpec((None, 4, BRANCH_WIDTH, tn), lambda i, j: (layer, 0, 0, j)),
                  pl.BlockSpec((4, 1, tn), lambda i, j: (0, 0, j)),
                  pl.BlockSpec((None, 4, 1, tn), lambda i, j: (layer, 0, 0, j))],
        out_specs=pl.BlockSpec((tm, tn), lambda i, j: (i, j)),
        out_shape=jax.ShapeDtypeStruct((n, D_MODEL), BF16),
        compiler_params=_cparams(("arbitrary", "arbitrary")),
    )(xb, sx, *outs, wg_all, wu_all, sc, b_all)


_HI16 = 0xFFFF0000


def _pack_bf16_pairs(val):
    half = val.shape[1] // 2
    lo = lax.bitcast_convert_type(val[:, :half].astype(BF16).astype(F32), jnp.uint32)
    hi = lax.bitcast_convert_type(val[:, half:].astype(BF16).astype(F32), jnp.uint32)
    return (lo >> 16) | (hi & jnp.uint32(_HI16))


def _unpack_bf16_pairs(words):
    lo = lax.bitcast_convert_type(words << 16, F32)
    hi = lax.bitcast_convert_type(words & jnp.uint32(_HI16), F32)
    return jnp.concatenate([lo, hi], axis=1)


def _to_rows(ref, val):
    m = val.shape[0]
    words = _pack_bf16_pairs(val)
    for j in range(ROW_CHUNKS):
        ref[pl.ds(j, m, stride=ROW_PITCH), :] = words[:, j * LANES:(j + 1) * LANES]
    for j in range(ROW_CHUNKS, ROW_PITCH):
        ref[pl.ds(j, m, stride=ROW_PITCH), :] = jnp.zeros((m, LANES), jnp.uint32)


def _from_rows(ref, m):
    words = jnp.concatenate([ref[pl.ds(j, m, stride=ROW_PITCH), :] for j in range(ROW_CHUNKS)], axis=1)
    return _unpack_bf16_pairs(words)


def _outln_kernel(m_ref, w_ref, x_ref, g_ref, b_ref, wr_ref, x1_ref, x1r_ref, lg_ref, *, alpha):
    y = alpha * x_ref[...] + _dot(m_ref[...], w_ref[...])
    x1 = _layer_norm(y, g_ref[...], b_ref[...])
    x1_ref[...] = x1
    _to_rows(x1r_ref, x1)
    lg_ref[...] = _dot_nt(wr_ref[...], x1.astype(BF16))


def _out_proj_ln(merged, w_all, x, g_all, b_all, wr_t, layer, alpha):
    n = x.shape[0]
    tm = _tile(n, 512)
    row = pl.BlockSpec((tm, D_MODEL), lambda i: (i, 0))
    vec = pl.BlockSpec((None, 1, D_MODEL), lambda i: (layer, 0, 0))
    return pl.pallas_call(
        functools.partial(_outln_kernel, alpha=alpha),
        grid=(n // tm,),
        in_specs=[row, pl.BlockSpec((None, D_MODEL, D_MODEL), lambda i: (layer, 0, 0)), row, vec, vec,
                  pl.BlockSpec((N_EXPERTS, D_MODEL), lambda i: (0, 0))],
        out_specs=[row, pl.BlockSpec((tm * ROW_PITCH, LANES), lambda i: (i, 0)),
                   pl.BlockSpec((N_EXPERTS, tm), lambda i: (0, i))],
        out_shape=[jax.ShapeDtypeStruct((n, D_MODEL), F32), jax.ShapeDtypeStruct((n * ROW_PITCH, LANES), jnp.uint32),
                   jax.ShapeDtypeStruct((N_EXPERTS, n), F32)],
        compiler_params=_cparams(("arbitrary",)),
    )(merged, w_all, x, g_all, b_all, wr_t)


def _route_kernel(lg_ref, bias_ref, idx_ref, wt_ref):
    s = _sigmoid(lg_ref[...])
    sel = s + bias_ref[...]
    tn = s.shape[1]
    rows = lax.broadcasted_iota(jnp.int32, (EXPERTS_PER_GROUP, tn), 0)
    neg = jnp.float32(-jnp.inf)

    def first_max(blk):
        m = jnp.max(blk, axis=0, keepdims=True)
        i = jnp.min(jnp.where(blk == m, rows, EXPERTS_PER_GROUP), axis=0, keepdims=True)
        return m, i

    best = None
    for g in range(N_GROUPS):
        sl = slice(g * EXPERTS_PER_GROUP, (g + 1) * EXPERTS_PER_GROUP)
        blk = sel[sl]
        sg = s[sl]
        m1, i1 = first_max(blk)
        m2, i2 = first_max(jnp.where(rows == i1, neg, blk))
        w1 = jnp.sum(jnp.where(rows == i1, sg, 0.0), axis=0, keepdims=True)
        w2 = jnp.sum(jnp.where(rows == i2, sg, 0.0), axis=0, keepdims=True)
        cand = (m1 + m2, i1 + g * EXPERTS_PER_GROUP, i2 + g * EXPERTS_PER_GROUP, w1, w2)
        if best is None:
            best = cand
        else:
            take = cand[0] > best[0]
            best = tuple(jnp.where(take, cn, bs) for cn, bs in zip(cand, best))
    _, e1, e2, w1, w2 = best
    tot = w1 + w2
    idx_ref[0:1, :] = e1
    idx_ref[1:2, :] = e2
    wt_ref[0:1, :] = w1 / tot
    wt_ref[1:2, :] = w2 / tot


def _route(logits_t, bias_col):
    n = logits_t.shape[1]
    tn = _tile(n, 2048)
    return pl.pallas_call(
        _route_kernel,
        grid=(n // tn,),
        in_specs=[pl.BlockSpec((N_EXPERTS, tn), lambda i: (0, i)),
                  pl.BlockSpec((N_EXPERTS, 1), lambda i: (0, 0))],
        out_specs=[pl.BlockSpec((2, tn), lambda i: (0, i))] * 2,
        out_shape=[jax.ShapeDtypeStruct((2, n), jnp.int32), jax.ShapeDtypeStruct((2, n), F32)],
        compiler_params=_cparams(("arbitrary",)),
    )(logits_t, bias_col)


def _rank_kernel(e_ref, u_ref, rank_ref, cnt_ref, carry_ref):
    @pl.when(pl.program_id(0) == 0)
    def _():
        carry_ref[...] = jnp.zeros_like(carry_ref)

    e = e_ref[...]
    tn = e.shape[1]
    rows = lax.broadcasted_iota(jnp.int32, (N_EXPERTS, tn), 0)
    onehot = jnp.where(rows == e, 1.0, 0.0)
    before = _dot(onehot.astype(BF16), u_ref[...])
    carry = carry_ref[...]
    rank_ref[...] = jnp.sum(onehot * (before + carry), axis=0, keepdims=True).astype(jnp.int32)
    carry = carry + jnp.sum(onehot, axis=1, keepdims=True)
    carry_ref[...] = carry
    cnt_ref[...] = carry


def _rank(e_row):
    m = e_row.shape[1]
    tn = _tile(m, RANK_TILE)
    upper = jnp.asarray(np.triu(np.ones((tn, tn), np.float32), 1), BF16)
    return pl.pallas_call(
        _rank_kernel,
        grid=(m // tn,),
        in_specs=[pl.BlockSpec((1, tn), lambda i: (0, i)), pl.BlockSpec((tn, tn), lambda i: (0, 0))],
        out_specs=[pl.BlockSpec((1, tn), lambda i: (0, i)), pl.BlockSpec((N_EXPERTS, 1), lambda i: (0, 0))],
        out_shape=[jax.ShapeDtypeStruct((1, m), jnp.int32), jax.ShapeDtypeStruct((N_EXPERTS, 1), F32)],
        scratch_shapes=[pltpu.VMEM((N_EXPERTS, 1), F32)],
        compiler_params=_cparams(("arbitrary",)),
    )(e_row, upper)


def _dispatch_plan(idx_t, n):
    m = 2 * n
    bm = MOE_BLOCK
    e_row = idx_t.reshape(1, m)
    rank, cnt = _rank(e_row)
    counts = cnt.reshape(N_EXPERTS).astype(jnp.int32)
    padded = (counts + bm - 1) // bm * bm
    pad_end = jnp.cumsum(padded)
    pad_start = pad_end - padded
    experts = jnp.arange(N_EXPERTS, dtype=jnp.int32)
    start_of = jnp.sum(jnp.where(e_row == experts[:, None], pad_start[:, None], 0), axis=0)
    dest = (rank.reshape(m) + start_of).astype(jnp.int32)
    n_rows = (m + N_EXPERTS * (bm - 1) + bm - 1) // bm * bm
    n_blocks = n_rows // bm
    starts = jnp.arange(n_blocks, dtype=jnp.int32) * bm
    block_e = jnp.minimum(jnp.sum((pad_end[None, :] <= starts[:, None]).astype(jnp.int32), axis=1),
                          N_EXPERTS - 1).astype(jnp.int32)
    n_used = (pad_end[-1] // bm).astype(jnp.int32).reshape(1)
    return dest, block_e, n_used, n_rows


def _scatter_kernel(dest_ref, x_ref, init_ref, out_ref, sem, *, tm, n):
    del init_ref
    base = pl.program_id(0) * tm

    def row_copy(r, k):
        return pltpu.make_async_copy(x_ref.at[pl.ds(r * ROW_PITCH, ROW_CHUNKS)],
                                     out_ref.at[pl.ds(dest_ref[k * n + base + r] * ROW_PITCH, ROW_CHUNKS)], sem)

    def start(r, carry):
        row_copy(r, 0).start(priority=0)
        row_copy(r, 1).start(priority=1)
        return carry

    lax.fori_loop(0, tm, start, 0, unroll=DMA_UNROLL)
    for _ in range(2):
        pltpu.make_async_copy(x_ref.at[pl.ds(0, tm * ROW_CHUNKS)], out_ref.at[pl.ds(0, tm * ROW_CHUNKS)], sem).wait()


def _scatter_rows(x_rows, dest, n_rows):
    n = x_rows.shape[0] // ROW_PITCH
    tm = _tile(n, SCATTER_TILE)
    init = jnp.zeros((n_rows * ROW_PITCH, LANES), x_rows.dtype)
    return pl.pallas_call(
        functools.partial(_scatter_kernel, tm=tm, n=n),
        grid_spec=pltpu.PrefetchScalarGridSpec(
            num_scalar_prefetch=1,
            grid=(n // tm,),
            in_specs=[pl.BlockSpec((tm * ROW_PITCH, LANES), lambda i, d: (i, 0)),
                      pl.BlockSpec(memory_space=pl.ANY)],
            out_specs=pl.BlockSpec(memory_space=pl.ANY),
            scratch_shapes=[pltpu.SemaphoreType.DMA],
        ),
        out_shape=jax.ShapeDtypeStruct((n_rows * ROW_PITCH, LANES), x_rows.dtype),
        input_output_aliases={2: 0},
        compiler_params=_cparams(("arbitrary",)),
    )(dest, x_rows, init)


def _moe_kernel(be_ref, nb_ref, x_ref, sx_ref, wgt_ref, wut_ref, wd_ref, y_ref, wg8_ref, wu8_ref, wd16_ref, sc_ref):
    i = pl.program_id(0)
    active = i < nb_ref[0]
    fresh = jnp.logical_or(i == 0, be_ref[i] != be_ref[jnp.maximum(i - 1, 0)])

    @pl.when(jnp.logical_and(active, fresh))
    def _():
        for k, (w_ref, w8_ref) in enumerate(((wgt_ref, wg8_ref), (wut_ref, wu8_ref))):
            w = w_ref[...]
            amax = jnp.max(jnp.max(jnp.abs(w), axis=1, keepdims=True), axis=0, keepdims=True)
            s = jnp.exp(jnp.floor(jnp.log(FP8_TARGET / jnp.maximum(amax, 1e-30)) * (1.0 / math.log(2.0)))
                        * math.log(2.0))
            w8_ref[...] = (w * s).astype(FP8)
            sc_ref[k:k + 1, :] = jnp.broadcast_to(1.0 / (s * sx_ref[...]), (1, LANES))
        wd16_ref[...] = wd_ref[...].astype(BF16)

    @pl.when(active)
    def _():
        x8 = (_from_rows(x_ref, MOE_BLOCK) * sx_ref[...]).astype(FP8)
        g = _dot_nt(x8, wg8_ref[...]) * sc_ref[0:1, 0:1]
        u = _dot_nt(x8, wu8_ref[...]) * sc_ref[1:2, 0:1]
        hid = ((g * _sigmoid(g)) * u).astype(BF16)
        _to_rows(y_ref, _dot(hid, wd16_ref[...]))

    @pl.when(i >= nb_ref[0])
    def _():
        y_ref[...] = jnp.zeros_like(y_ref)


def _moe_ffn(xs, sx, block_e, n_used, wgt_all, wut_all, wd_all, layer):
    n_rows = xs.shape[0] // ROW_PITCH
    bm = MOE_BLOCK
    rows = pl.BlockSpec((bm * ROW_PITCH, LANES), lambda i, be, nb: (i, 0))
    wspec = pl.BlockSpec((None, None, D_EXPERT, D_MODEL), lambda i, be, nb: (layer, be[i], 0, 0))
    return pl.pallas_call(
        _moe_kernel,
        grid_spec=pltpu.PrefetchScalarGridSpec(
            num_scalar_prefetch=2,
            grid=(n_rows // bm,),
            in_specs=[rows, pl.BlockSpec((1, 1), lambda i, be, nb: (0, 0)), wspec, wspec, wspec],
            out_specs=rows,
            scratch_shapes=[pltpu.VMEM((D_EXPERT, D_MODEL), FP8), pltpu.VMEM((D_EXPERT, D_MODEL), FP8),
                            pltpu.VMEM((D_EXPERT, D_MODEL), BF16), pltpu.VMEM((SUBLANES, LANES), F32)],
        ),
        out_shape=jax.ShapeDtypeStruct((n_rows * ROW_PITCH, LANES), jnp.uint32),
        compiler_params=_cparams(("arbitrary",)),
    )(block_e, n_used, xs, sx, wgt_all, wut_all, wd_all)


def _combine_kernel(dest_ref, x_ref, w_ref, yb_ref, g_ref, b_ref, o_ref, ob_ref, buf_ref, sem, *, tm, n, alpha):
    i = pl.program_id(0)
    half = i % 2

    def gather(step, into):
        base = step * tm

        def start(r, carry):
            for k in range(2):
                pltpu.make_async_copy(yb_ref.at[pl.ds(dest_ref[k * n + base + r] * ROW_PITCH, ROW_CHUNKS)],
                                      buf_ref.at[into, k, pl.ds(r * ROW_PITCH, ROW_CHUNKS)],
                                      sem.at[into]).start(priority=k)
            return carry

        lax.fori_loop(0, tm, start, 0, unroll=DMA_UNROLL)

    @pl.when(i == 0)
    def _():
        gather(0, 0)

    @pl.when(i + 1 < pl.num_programs(0))
    def _():
        gather(i + 1, 1 - half)

    for k in range(2):
        pltpu.make_async_copy(yb_ref.at[pl.ds(0, tm * ROW_CHUNKS)], buf_ref.at[half, k, pl.ds(0, tm * ROW_CHUNKS)],
                              sem.at[half]).wait()
    w = w_ref[...]
    y = w[:, 0:1] * _from_rows(buf_ref.at[half, 0], tm) + w[:, 1:2] * _from_rows(buf_ref.at[half, 1], tm)
    x2 = _layer_norm(alpha * x_ref[...] + y, g_ref[...], b_ref[...])
    o_ref[...] = x2
    ob_ref[...] = x2.astype(BF16)


def _combine_ln(x1, w_col, yb, dest, g_all, b_all, layer, alpha):
    n = x1.shape[0]
    tm = _tile(n, COMBINE_TILE)
    row = pl.BlockSpec((tm, D_MODEL), lambda i, d: (i, 0))
    vec = pl.BlockSpec((None, 1, D_MODEL), lambda i, d: (layer, 0, 0))
    return pl.pallas_call(
        functools.partial(_combine_kernel, tm=tm, n=n, alpha=alpha),
        grid_spec=pltpu.PrefetchScalarGridSpec(
            num_scalar_prefetch=1,
            grid=(n // tm,),
            in_specs=[row, pl.BlockSpec((tm, 2), lambda i, d: (i, 0)), pl.BlockSpec(memory_space=pl.ANY), vec, vec],
            out_specs=[row, row],
            scratch_shapes=[pltpu.VMEM((2, 2, tm * ROW_PITCH, LANES), jnp.uint32), pltpu.SemaphoreType.DMA((2,))],
        ),
        out_shape=[jax.ShapeDtypeStruct((n, D_MODEL), F32), jax.ShapeDtypeStruct((n, D_MODEL), BF16)],
        compiler_params=_cparams(("arbitrary",)),
    )(dest, x1, w_col, yb, g_all, b_all)


def _relayout_w_in(w_in):
    wt = jnp.swapaxes(w_in, 1, 2)

    def seg(name):
        a, b = _ORIG[name]
        return wt[:, a:b, :]

    def partner(name):
        a0 = _ORIG[name][0]
        half = RET_DK // 2
        parts = []
        for h in range(N_HEADS):
            b = a0 + RET_DK * h
            parts += [-wt[:, b + half:b + RET_DK, :], wt[:, b:b + half, :]]
        return parts

    pad = jnp.zeros((wt.shape[0], LANES - GLA_RANK + _H_PAD, wt.shape[2]), wt.dtype)
    rows = ([seg(nm) for nm in _SEG512] + [seg("gq"), seg("gk"), seg("rq"), seg("rk")]
            + partner("rq") + partner("rk") + [seg("glr"), pad])
    return jnp.concatenate(rows, axis=1).astype(BF16)


def kernel(x, positions, w_in, w_gla_gate2, b_gla_gate2, gla_norm, hgrn_lb, hgrn_norm, conv_w, conv_b,
           w_lru_a, b_lru_a, w_lru_x, b_lru_x, lru_lambda, w_branch_up, w_merge_gate, b_merge_gate, w_out,
           ln1_g, ln1_b, w_router, router_bias, w_exp_gate, w_exp_up, w_exp_down, ln2_g, ln2_b):
    bsz, seq, d = x.shape
    depth = w_in.shape[0]
    n = bsz * seq
    alpha = (2 * depth) ** 0.25

    w_in_b = _relayout_w_in(w_in)
    w2_b = jnp.pad(w_gla_gate2, ((0, 0), (0, LANES - GLA_RANK), (0, 0))).astype(BF16)
    b2 = b_gla_gate2.reshape(depth, 1, -1)
    gla_nw = jnp.tile(gla_norm, (1, N_HEADS)).reshape(depth, 1, -1)
    hgrn_nw = jnp.tile(hgrn_norm, (1, N_HEADS)).reshape(depth, 1, -1)
    lb_cum = jnp.cumsum(jax.nn.softmax(hgrn_lb.astype(F32), axis=0), axis=0)
    lower = lb_cum - lb_cum[:1]
    log_lb = jnp.log(lower).reshape(depth, 1, -1)
    log_1m_lb = jnp.log1p(-lower).reshape(depth, 1, -1)
    one_m_lb = (1.0 - lower).reshape(depth, 1, -1)
    wa_b = w_lru_a.astype(BF16)
    wx_b = w_lru_x.astype(BF16)
    sw = _pow2_scale(jnp.max(jnp.abs(w_merge_gate), axis=(2, 3)))
    wg_q = (w_merge_gate * sw[:, :, None, None]).astype(FP8)
    ln_bound = math.sqrt(d) * jnp.max(jnp.abs(ln2_g), axis=1) + jnp.max(jnp.abs(ln2_b), axis=1)
    x_bound = jnp.concatenate([jnp.max(jnp.abs(x)).reshape(1), ln_bound[:depth - 1]])
    sx_all = _pow2_scale(x_bound)
    sx1_all = _pow2_scale(math.sqrt(d) * jnp.max(jnp.abs(ln1_g), axis=1) + jnp.max(jnp.abs(ln1_b), axis=1))
    sc_all = jnp.broadcast_to((1.0 / (sw * sx_all[:, None]))[:, :, None, None], (depth, 4, 1, d)).astype(F32)
    wu_b = w_branch_up.astype(BF16)
    bmg = b_merge_gate.reshape(depth, 4, 1, d)
    w_out_b = w_out.astype(BF16)
    wr_t = w_router.T.astype(BF16)
    bias_col = router_bias.reshape(N_EXPERTS, 1).astype(F32)
    ln1g = ln1_g.reshape(depth, 1, d)
    ln1b = ln1_b.reshape(depth, 1, d)
    ln2g = ln2_g.reshape(depth, 1, d)
    ln2b = ln2_b.reshape(depth, 1, d)
    wgt = jnp.swapaxes(w_exp_gate, 2, 3)
    wut = jnp.swapaxes(w_exp_up, 2, 3)

    half = RET_DK // 2
    inv_freq = ROPE_BASE ** (-jnp.arange(half, dtype=F32) / half)
    invf = jnp.tile(inv_freq, 2 * N_HEADS).reshape(1, N_HEADS * RET_DK)
    cos_t, sin_t = _rope_tables(positions.reshape(n, 1), invf)
    cos_t = cos_t.reshape(bsz, seq, -1)
    sin_t = sin_t.reshape(bsz, seq, -1)

    cum, lv = _gated_consts(_tile(seq, GATED_BLOCK))
    ret_c = _tile(seq, RET_BLOCK)
    ret_consts = _ret_consts(ret_c)

    x2d = x.reshape(n, d)
    xb = x2d.astype(BF16)
    for l in range(depth):
        hcat = _in_proj(xb, w_in_b, l).reshape(bsz, seq, H_COLS)
        o_gla = _gla_branch(hcat, w2_b[l], b2[l], gla_nw[l], cum, lv)
        o_hgrn = _hgrn_branch(hcat, log_lb[l], log_1m_lb[l], one_m_lb[l], hgrn_nw[l], cum, lv)
        o_ret = _ret_branch(hcat, cos_t, sin_t, ret_consts, ret_c)
        o_lru = _lru_branch(hcat, conv_w[l], conv_b[l].reshape(1, -1), wa_b[l], b_lru_a[l].reshape(1, -1),
                            wx_b[l], b_lru_x[l].reshape(1, -1), lru_lambda[l].reshape(1, -1))
        outs = tuple(o.reshape(n, BRANCH_WIDTH) for o in (o_gla, o_hgrn, o_ret, o_lru))
        merged = _merge(xb, sx_all[l].reshape(1, 1), outs, wg_q, wu_b, sc_all[l], bmg, l)
        x1, x1_rows, logits_t = _out_proj_ln(merged, w_out_b, x2d, ln1g, ln1b, wr_t, l, alpha)
        idx_t, wt_t = _route(logits_t, bias_col)
        dest, block_e, n_used, n_rows = _dispatch_plan(idx_t, n)
        xs = _scatter_rows(x1_rows, dest, n_rows)
        yb = _moe_ffn(xs, sx1_all[l].reshape(1, 1), block_e, n_used, wgt, wut, w_exp_down, l)
        x2d, xb = _combine_ln(x1, wt_t.T, yb, dest, ln2g, ln2b, l, alpha)
    return x2d.reshape(bsz, seq, d)
```
